```python
import math
import jax, jax.numpy as jnp
from jax import lax
import numpy as np

D_MODEL = 1024
BATCH = 2
SEQ = 16384
DEPTH = 2

SSM_WIDTH = 512
SSM_GROUP = 16
SSM_GROUPS = SSM_WIDTH // SSM_GROUP
SSM_STATE = 64
STEP_MIN = 0.001
STEP_MAX = 0.1
RET_HEADS = 4
RET_DK = 128
RET_DV = 256
RET_QK_WIDTH = RET_HEADS * RET_DK
RET_V_WIDTH = RET_HEADS * RET_DV
RET_CHUNK = 128
ROPE_BASE = 10000.0
D_FF = 4 * D_MODEL
EPS = 1e-6
IN_WIDTH = SSM_WIDTH + 2 * RET_QK_WIDTH + 2 * RET_V_WIDTH + 2 * D_MODEL
IN_SPLITS = (
    SSM_WIDTH,
    SSM_WIDTH + RET_QK_WIDTH,
    SSM_WIDTH + 2 * RET_QK_WIDTH,
    SSM_WIDTH + 2 * RET_QK_WIDTH + RET_V_WIDTH,
    SSM_WIDTH + 2 * RET_QK_WIDTH + 2 * RET_V_WIDTH,
    SSM_WIDTH + 2 * RET_QK_WIDTH + 2 * RET_V_WIDTH + D_MODEL,
)

kernel_name = "hybrid_s5_retention_gated_trunk"


def rmsnorm(x, g):
    x32 = x.astype(jnp.float32)
    y = x32 * lax.rsqrt(jnp.mean(x32 * x32, axis=-1, keepdims=True) + EPS)
    return (y * g.astype(jnp.float32)).astype(x.dtype)


def cmul(ar, ai, br, bi):
    return ar * br - ai * bi, ar * bi + ai * br


def s5_branch(u, lam_re, lam_im, b_re, b_im, c_re, c_im, d_skip, log_step, w_glu, b_glu):
    bsz, seq, _ = u.shape
    f32 = jnp.float32
    u32 = u.astype(f32)
    ug = u32.reshape(bsz, seq, SSM_GROUPS, SSM_GROUP)
    lr = lam_re.astype(f32)
    li = lam_im.astype(f32)
    step = jnp.exp(log_step.astype(f32))[:, None]
    mag = jnp.exp(lr * step)
    a_r = mag * jnp.cos(li * step)
    a_i = mag * jnp.sin(li * step)
    nr, ni = a_r - 1.0, a_i
    den = lr * lr + li * li
    kr = (nr * lr + ni * li) / den
    ki = (ni * lr - nr * li) / den
    bb_r, bb_i = cmul(kr[..., None], ki[..., None], b_re.astype(f32), b_im.astype(f32))
    bu_r = jnp.einsum('blgc,gpc->blgp', ug, bb_r)
    bu_i = jnp.einsum('blgc,gpc->blgp', ug, bb_i)
    a_r_seq = jnp.broadcast_to(a_r[None, None], (1, seq, SSM_GROUPS, SSM_STATE))
    a_i_seq = jnp.broadcast_to(a_i[None, None], (1, seq, SSM_GROUPS, SSM_STATE))

    def combine(e1, e2):
        a1r, a1i, b1r, b1i = e1
        a2r, a2i, b2r, b2i = e2
        ar_, ai_ = cmul(a2r, a2i, a1r, a1i)
        br_, bi_ = cmul(a2r, a2i, b1r, b1i)
        return ar_, ai_, br_ + b2r, bi_ + b2i

    _, _, h_r, h_i = lax.associative_scan(combine, (a_r_seq, a_i_seq, bu_r, bu_i), axis=1)
    y = (jnp.einsum('gcp,blgp->blgc', c_re.astype(f32), h_r)
         - jnp.einsum('gcp,blgp->blgc', c_im.astype(f32), h_i))
    y = y.reshape(bsz, seq, SSM_WIDTH) + d_skip.astype(f32) * u32
    y = jax.nn.gelu(y)
    y = y * jax.nn.sigmoid(y @ w_glu.astype(f32) + b_glu.astype(f32))
    return y.astype(u.dtype)


def rotary(t, pos):
    dk = t.shape[-1]
    inv_freq = ROPE_BASE ** (-jnp.arange(0, dk, 2, dtype=jnp.float32) / dk)
    ang = pos[:, None] * inv_freq[None, :]
    cos, sin = jnp.cos(ang), jnp.sin(ang)
    t1, t2 = t[..., : dk // 2], t[..., dk // 2:]
    return jnp.concatenate([t1 * cos - t2 * sin, t1 * sin + t2 * cos], axis=-1)


def retention_branch(q, k, v, g):
    bsz, seq, _ = q.shape
    f32 = jnp.float32
    n_chunks = seq // RET_CHUNK
    pos = jnp.arange(seq, dtype=f32)
    qh = q.astype(f32).reshape(bsz, seq, RET_HEADS, RET_DK).transpose(0, 2, 1, 3)
    kh = k.astype(f32).reshape(bsz, seq, RET_HEADS, RET_DK).transpose(0, 2, 1, 3)
    vh = v.astype(f32).reshape(bsz, seq, RET_HEADS, RET_DV).transpose(0, 2, 1, 3)
    qh = rotary(qh, pos)
    kh = rotary(kh, pos) * (RET_DK ** -0.5)
    qc = qh.reshape(bsz, RET_HEADS, n_chunks, RET_CHUNK, RET_DK)
    kc = kh.reshape(bsz, RET_HEADS, n_chunks, RET_CHUNK, RET_DK)
    vc = vh.reshape(bsz, RET_HEADS, n_chunks, RET_CHUNK, RET_DV)
    log_gamma = jnp.log(1.0 - 2.0 ** (-5.0 - jnp.arange(RET_HEADS, dtype=f32)))
    idx = jnp.arange(RET_CHUNK, dtype=f32)
    rel = idx[:, None] - idx[None, :]
    inner_decay = jnp.where(rel[None] >= 0,
                            jnp.exp(jnp.maximum(rel, 0.0)[None] * log_gamma[:, None, None]),
                            0.0)
    scores = jnp.einsum('bhncd,bhnsd->bhncs', qc, kc) * inner_decay[None, :, None]
    inner = jnp.einsum('bhncs,bhnse->bhnce', scores, vc)
    zeta = jnp.exp((RET_CHUNK - 1.0 - idx)[None, :] * log_gamma[:, None])
    kv = jnp.einsum('bhnsd,hs,bhnse->bhnde', kc, zeta, vc)
    chunk_decay = jnp.exp(RET_CHUNK * log_gamma)[None, :, None, None]

    def step(state, kv_i):
        return chunk_decay * state + kv_i, state

    _, r_prev = lax.scan(step, jnp.zeros((bsz, RET_HEADS, RET_DK, RET_DV), f32),
                         jnp.moveaxis(kv, 2, 0))
    r_prev = jnp.moveaxis(r_prev, 0, 2)
    xi = jnp.exp((idx + 1.0)[None, :] * log_gamma[:, None])
    cross = jnp.einsum('bhncd,hc,bhnde->bhnce', qc, xi, r_prev)
    o = (inner + cross).reshape(bsz, RET_HEADS, seq, RET_DV)
    o = o * lax.rsqrt(jnp.mean(o * o, axis=-1, keepdims=True) + EPS)
    o = o.transpose(0, 2, 1, 3).reshape(bsz, seq, RET_V_WIDTH)
    return (o * jax.nn.silu(g.astype(f32))).astype(q.dtype)


def setup_inputs(seed: int = 0) -> dict:
    key = jax.random.key(seed)
    ks = jax.random.split(key, 24)
    f32 = jnp.float32
    nrm = lambda k, shape, scale: jax.random.normal(k, shape, f32) * scale
    G, P, C = SSM_GROUPS, SSM_STATE, SSM_GROUP
    x = jax.random.normal(ks[0], (BATCH, SEQ, D_MODEL), f32)
    w_in = nrm(ks[1], (DEPTH, D_MODEL, IN_WIDTH), D_MODEL ** -0.5)
    lam_re = -0.5 + nrm(ks[2], (DEPTH, G, P), 0.01)
    lam_im = math.pi * jnp.broadcast_to(jnp.arange(P, dtype=f32), (DEPTH, G, P)) + nrm(ks[3], (DEPTH, G, P), 0.01)
    b_re = nrm(ks[4], (DEPTH, G, P, C), (2.0 * C) ** -0.5)
    b_im = nrm(ks[5], (DEPTH, G, P, C), (2.0 * C) ** -0.5)
    c_re = nrm(ks[6], (DEPTH, G, C, P), (2.0 * P) ** -0.5)
    c_im = nrm(ks[7], (DEPTH, G, C, P), (2.0 * P) ** -0.5)
    d_skip = nrm(ks[8], (DEPTH, SSM_WIDTH), 1.0)
    log_step = jax.random.uniform(ks[9], (DEPTH, G), f32, math.log(STEP_MIN), math.log(STEP_MAX))
    w_glu = nrm(ks[10], (DEPTH, SSM_WIDTH, SSM_WIDTH), SSM_WIDTH ** -0.5)
    b_glu = nrm(ks[11], (DEPTH, SSM_WIDTH), 0.01)
    w_proj_ssm = nrm(ks[12], (DEPTH, SSM_WIDTH, D_MODEL), SSM_WIDTH ** -0.5)
    w_proj_ret = nrm(ks[13], (DEPTH, RET_V_WIDTH, D_MODEL), RET_V_WIDTH ** -0.5)
    w_out = nrm(ks[14], (DEPTH, D_MODEL, D_MODEL), D_MODEL ** -0.5)
    norm_mix = 1.0 + nrm(ks[15], (DEPTH, D_MODEL), 0.02)
    norm_mlp = 1.0 + nrm(ks[16], (DEPTH, D_MODEL), 0.02)
    w_up = nrm(ks[17], (DEPTH, D_MODEL, D_FF), D_MODEL ** -0.5)
    w_down = nrm(ks[18], (DEPTH, D_FF, D_MODEL), D_FF ** -0.5)
    norm_final = 1.0 + nrm(ks[19], (D_MODEL,), 0.02)
    return {"x": x, "w_in": w_in, "lam_re": lam_re, "lam_im": lam_im,
            "b_re": b_re, "b_im": b_im, "c_re": c_re, "c_im": c_im,
            "d_skip": d_skip, "log_step": log_step, "w_glu": w_glu, "b_glu": b_glu,
            "w_proj_ssm": w_proj_ssm, "w_proj_ret": w_proj_ret, "w_out": w_out,
            "norm_mix": norm_mix, "norm_mlp": norm_mlp, "w_up": w_up, "w_down": w_down,
            "norm_final": norm_final}


def reference(x, w_in, lam_re, lam_im, b_re, b_im, c_re, c_im, d_skip, log_step,
              w_glu, b_glu, w_proj_ssm, w_proj_ret, w_out, norm_mix, norm_mlp,
              w_up, w_down, norm_final):
    h = x
    for l in range(DEPTH):
        z = rmsnorm(h, norm_mix[l])
        proj = z @ w_in[l]
        u, q, k, v, g, gate_a, gate_b = jnp.split(proj, IN_SPLITS, axis=-1)
        y_a = s5_branch(u, lam_re[l], lam_im[l], b_re[l], b_im[l], c_re[l], c_im[l],
                        d_skip[l], log_step[l], w_glu[l], b_glu[l])
        y_b = retention_branch(q, k, v, g)
        merged = (jax.nn.sigmoid(gate_a) * (y_a @ w_proj_ssm[l])
                  + jax.nn.sigmoid(gate_b) * (y_b @ w_proj_ret[l]))
        h = h + merged @ w_out[l]
        z = rmsnorm(h, norm_mlp[l])
        h = h + jnp.square(jax.nn.relu(z @ w_up[l])) @ w_down[l]
    return rmsnorm(h, norm_final)
```

```python
import functools
import math

import numpy as np
import jax
import jax.numpy as jnp
from jax import lax
from jax.experimental import pallas as pl
from jax.experimental.pallas import tpu as pltpu

F32 = jnp.float32
BF16 = jnp.bfloat16

EPS = 1e-6
SSM_GROUP = 16
RET_HEADS = 4
RET_DK = 128
RET_DV = 256
ROPE_BASE = 10000.0

LANES = 128
S5_CHUNK = LANES
RET_CHUNK = 256
TOKEN_TILE = 512
VMEM_LIMIT = 56 * 1024 * 1024


def _params(n_axes, vmem=VMEM_LIMIT):
    return pltpu.CompilerParams(
        dimension_semantics=("arbitrary",) * n_axes, vmem_limit_bytes=vmem)


def _whole(shape):
    zeros = (0,) * len(shape)
    return pl.BlockSpec(shape, lambda *_: zeros, pipeline_mode=pl.Buffered(1))


def _dot(a, b):
    return jnp.dot(a, b, preferred_element_type=F32)


def _dot_nt(a, b):
    return lax.dot_general(a, b, (((1,), (1,)), ((), ())), preferred_element_type=F32)


def _dot_tn(a, b):
    return lax.dot_general(a, b, (((0,), (0,)), ((), ())), preferred_element_type=F32)


def _rmsnorm(x, w):
    return x * lax.rsqrt(jnp.mean(x * x, axis=-1, keepdims=True) + EPS) * w


def _rot_half(x, c, s):
    return x * c + pltpu.roll(x, LANES // 2, 1) * s


def _in_proj_body(h_ref, nw_ref, wuT_ref, wq_ref, wk_ref, wr_ref, cos_ref, sin_ref,
                  uT_ref, q_ref, k_ref, v_ref, g_ref, ga_ref, gb_ref):
    z = _rmsnorm(h_ref[...], nw_ref[...]).astype(BF16)
    uT_ref[...] = _dot_nt(wuT_ref[...], z).astype(BF16)
    cos = cos_ref[...]
    sin = sin_ref[...]
    k_scale = RET_DK ** -0.5
    for j in range(RET_HEADS):
        cols = slice(j * RET_DK, (j + 1) * RET_DK)
        q_ref[:, cols] = _rot_half(_dot(z, wq_ref[:, cols]), cos, sin).astype(BF16)
        k_ref[:, cols] = (_rot_half(_dot(z, wk_ref[:, cols]), cos, sin) * k_scale).astype(BF16)
    width = v_ref.shape[1]
    for j, o_ref in enumerate((v_ref, g_ref, ga_ref, gb_ref)):
        o_ref[...] = _dot(z, wr_ref[:, j * width:(j + 1) * width]).astype(BF16)


def _in_proj(h, nw, wuT, wq, wk, wr, cos, sin, batch, seq):
    n, d = h.shape
    tm = min(TOKEN_TILE, seq)
    nt = seq // tm
    sw = wuT.shape[0]
    qk = wq.shape[1]
    vw = wr.shape[1] // 4
    row = lambda w: pl.BlockSpec((tm, w), lambda i: (i, 0))
    tab = pl.BlockSpec((tm, LANES), lambda i: (i % nt, 0))
    out_shape = (
        jax.ShapeDtypeStruct((batch, sw, seq), BF16),
        jax.ShapeDtypeStruct((n, qk), BF16), jax.ShapeDtypeStruct((n, qk), BF16),
        jax.ShapeDtypeStruct((n, vw), BF16), jax.ShapeDtypeStruct((n, vw), BF16),
        jax.ShapeDtypeStruct((n, vw), BF16), jax.ShapeDtypeStruct((n, vw), BF16),
    )
    return pl.pallas_call(
        _in_proj_body,
        grid=(n // tm,),
        in_specs=[row(d), _whole(nw.shape), _whole(wuT.shape), _whole(wq.shape),
                  _whole(wk.shape), _whole(wr.shape), tab, tab],
        out_specs=(pl.BlockSpec((None, sw, tm), lambda i: (i // nt, 0, i % nt)),
                   row(qk), row(qk), row(vw), row(vw), row(vw), row(vw)),
        out_shape=out_shape,
        compiler_params=_params(1),
        name="in_proj",
    )(h, nw, wuT, wq, wk, wr, cos, sin)


def _toeplitz_body(g1r_ref, g1i_ref, apr_ref, api_ref, m_ref, taps_ref):
    hi = lax.Precision.HIGHEST
    taps_ref[...] = (jnp.dot(g1r_ref[...], apr_ref[...], precision=hi, preferred_element_type=F32)
                     - jnp.dot(g1i_ref[...], api_ref[...], precision=hi, preferred_element_type=F32))
    t = taps_ref.shape[1]
    causal = (lax.broadcasted_iota(jnp.int32, (t, t), 1)
              >= lax.broadcasted_iota(jnp.int32, (t, t), 0))

    def per_input_channel(ci, carry):
        for co in range(SSM_GROUP):
            row = taps_ref[pl.ds(ci * SSM_GROUP + co, 1), :]
            blk = pltpu.roll(jnp.broadcast_to(row, (t, t)), 0, 1, stride=1, stride_axis=0)
            m_ref[pl.ds(pl.multiple_of(ci * t, t), t), co * t:(co + 1) * t] = (
                jnp.where(causal, blk, 0.0).astype(BF16))
        return carry

    lax.fori_loop(0, SSM_GROUP, per_input_channel, 0)


def _toeplitz(g1r, g1i, apr, api):
    groups, cc, p = g1r.shape
    t = apr.shape[2]
    width = SSM_GROUP * t
    return pl.pallas_call(
        _toeplitz_body,
        grid=(groups,),
        in_specs=[pl.BlockSpec((None, cc, p), lambda g: (g, 0, 0)),
                  pl.BlockSpec((None, cc, p), lambda g: (g, 0, 0)),
                  pl.BlockSpec((None, p, t), lambda g: (g, 0, 0)),
                  pl.BlockSpec((None, p, t), lambda g: (g, 0, 0))],
        out_specs=pl.BlockSpec((None, width, width), lambda g: (g, 0, 0)),
        out_shape=jax.ShapeDtypeStruct((groups, width, width), BF16),
        scratch_shapes=[pltpu.VMEM((cc, t), F32)],
        compiler_params=_params(1),
        name="s5_toeplitz",
    )(g1r, g1i, apr, api)


def _s5_body(u_ref, m_ref, bs_ref, cx_ref, d_ref, wc_ref, ws_ref, y_ref):
    batch, _, n, t = u_ref.shape
    u = jnp.concatenate(
        [jnp.concatenate([u_ref[b, c] for c in range(SSM_GROUP)], axis=1) for b in range(batch)],
        axis=0)
    x = _dot(u, bs_ref[...])
    rows = lax.broadcasted_iota(jnp.int32, x.shape, 0) % n
    for step in range(wc_ref.shape[0]):
        sh = 1 << step
        prev = jnp.where(rows >= sh, pltpu.roll(x, sh, 0), 0.0)
        x = x + _rot_half(prev, wc_ref[step:step + 1, :], ws_ref[step:step + 1, :])
    h_prev = jnp.where(rows >= 1, pltpu.roll(x, 1, 0), 0.0).astype(BF16)
    y = _dot(u, m_ref[...]) + _dot(h_prev, cx_ref[...]) + d_ref[...] * u.astype(F32)
    y = jax.nn.gelu(y).astype(BF16)
    for b in range(batch):
        for c in range(SSM_GROUP):
            y_ref[b, c] = y[b * n:(b + 1) * n, c * t:(c + 1) * t]


def _s5(uT4, m, bs, cx, drow, wc, ws):
    batch, sw, n, t = uT4.shape
    groups = sw // SSM_GROUP
    width = SSM_GROUP * t
    grp = lambda *shape: pl.BlockSpec((None,) + shape, lambda g: (g,) + (0,) * len(shape))
    io = pl.BlockSpec((batch, SSM_GROUP, n, t), lambda g: (0, g, 0, 0))
    return pl.pallas_call(
        _s5_body,
        grid=(groups,),
        in_specs=[io, grp(width, width), grp(width, LANES), grp(LANES, width),
                  grp(1, width), grp(*wc.shape[1:]), grp(*ws.shape[1:])],
        out_specs=io,
        out_shape=jax.ShapeDtypeStruct(uT4.shape, BF16),
        compiler_params=_params(1),
        name="s5",
    )(uT4, m, bs, cx, drow, wc, ws)


def _retention_body(q_ref, k_ref, v_ref, g_ref, dm_ref, zeta_ref, xi_ref, o_ref, state_ref, *,
                    chunk_decay):
    @pl.when(pl.program_id(1) == 0)
    def _():
        state_ref[...] = jnp.zeros_like(state_ref)

    for hh in range(RET_HEADS):
        qk_cols = slice(hh * RET_DK, (hh + 1) * RET_DK)
        v_cols = slice(hh * RET_DV, (hh + 1) * RET_DV)
        q = q_ref[:, qk_cols]
        k = k_ref[:, qk_cols]
        v = v_ref[:, v_cols]
        state = state_ref[hh]
        scores = _dot_nt(q, k) * dm_ref[hh]
        o = _dot(scores.astype(BF16), v) + xi_ref[hh] * _dot(q, state.astype(BF16))
        kz = (k.astype(F32) * zeta_ref[hh]).astype(BF16)
        state_ref[hh] = chunk_decay[hh] * state + _dot_tn(kz, v)
        o = o * lax.rsqrt(jnp.mean(o * o, axis=-1, keepdims=True) + EPS)
        o_ref[:, v_cols] = (o * jax.nn.silu(g_ref[:, v_cols].astype(F32))).astype(BF16)


def _retention(q, k, v, g, batch, seq):
    n = q.shape[0]
    c = min(RET_CHUNK, seq)
    nc = seq // c
    gamma = 1.0 - 2.0 ** (-5.0 - np.arange(RET_HEADS, dtype=np.float64))
    idx = np.arange(c, dtype=np.float64)
    rel = idx[:, None] - idx[None, :]
    dmask = np.where(rel >= 0, gamma[:, None, None] ** np.maximum(rel, 0.0), 0.0)
    zeta = np.broadcast_to((gamma[:, None] ** (c - 1.0 - idx))[:, :, None], (RET_HEADS, c, RET_DK))
    xi = np.broadcast_to((gamma[:, None] ** (idx + 1.0))[:, :, None], (RET_HEADS, c, RET_DV))
    chunk_decay = tuple(float(x) for x in gamma ** c)
    row = lambda w: pl.BlockSpec((c, w), lambda b, j: (b * nc + j, 0))
    return pl.pallas_call(
        functools.partial(_retention_body, chunk_decay=chunk_decay),
        grid=(batch, nc),
        in_specs=[row(q.shape[1]), row(k.shape[1]), row(v.shape[1]), row(g.shape[1]),
                  _whole(dmask.shape), _whole(zeta.shape), _whole(xi.shape)],
        out_specs=row(v.shape[1]),
        out_shape=jax.ShapeDtypeStruct((n, v.shape[1]), BF16),
        scratch_shapes=[pltpu.VMEM((RET_HEADS, RET_DK, RET_DV), F32)],
        compiler_params=_params(2),
        name="retention",
    )(q, k, v, g, jnp.asarray(dmask, F32), jnp.asarray(zeta, F32), jnp.asarray(xi, F32))


def _mix_ffn_body(h_ref, yT_ref, yb_ref, ga_ref, gb_ref, wgT_ref, bg_ref, wpa_ref, wpb_ref,
                  wo_ref, nm_ref, wup_ref, wdn_ref, nf_ref, o_ref, *, final):
    yT = yT_ref[...]
    gate = jax.nn.sigmoid(_dot(wgT_ref[...], yT) + bg_ref[...])
    yaT = (yT.astype(F32) * gate).astype(BF16)
    merged = (jax.nn.sigmoid(ga_ref[...].astype(F32)) * _dot_tn(yaT, wpa_ref[...])
              + jax.nn.sigmoid(gb_ref[...].astype(F32)) * _dot(yb_ref[...], wpb_ref[...]))
    h = h_ref[...] + _dot(merged.astype(BF16), wo_ref[...])
    z = _rmsnorm(h, nm_ref[...]).astype(BF16)
    ff = wup_ref.shape[1]
    step = 1024
    for j in range(0, ff, step):
        a = jnp.square(jnp.maximum(_dot(z, wup_ref[:, j:j + step]), 0.0)).astype(BF16)
        h = h + _dot(a, wdn_ref[j:j + step, :])
    if final:
        h = _rmsnorm(h, nf_ref[...])
    o_ref[...] = h


def _mix_ffn(h, yT, yb, ga, gb, wgT, bg, wpa, wpb, wo, nm, wup, wdn, nf, batch, seq, final):
    n, d = h.shape
    tm = min(TOKEN_TILE, seq)
    nt = seq // tm
    sw = yT.shape[1]
    row = lambda w: pl.BlockSpec((tm, w), lambda i: (i, 0))
    consts = (wgT, bg, wpa, wpb, wo, nm, wup, wdn, nf)
    return pl.pallas_call(
        functools.partial(_mix_ffn_body, final=final),
        grid=(n // tm,),
        in_specs=[row(d), pl.BlockSpec((None, sw, tm), lambda i: (i // nt, 0, i % nt)),
                  row(yb.shape[1]), row(d), row(d)] + [_whole(w.shape) for w in consts],
        out_specs=row(d),
        out_shape=jax.ShapeDtypeStruct((n, d), F32),
        compiler_params=_params(1),
        name="mix_ffn",
    )(h, yT, yb, ga, gb, *consts)


def _cmul(ar, ai, br, bi):
    return ar * br - ai * bi, ar * bi + ai * br


def _s5_tables(lam_re, lam_im, b_re, b_im, c_re, c_im, d_skip, log_step, t, n_chunks):
    lr, li = lam_re.astype(F32), lam_im.astype(F32)
    step = jnp.exp(log_step.astype(F32))[:, None]
    mag = jnp.exp(lr * step)
    a_r, a_i = mag * jnp.cos(li * step), mag * jnp.sin(li * step)
    nr, ni = a_r - 1.0, a_i
    den = lr * lr + li * li
    kr = (nr * lr + ni * li) / den
    ki = (ni * lr - nr * li) / den
    bb_r, bb_i = _cmul(kr[..., None], ki[..., None], b_re.astype(F32), b_im.astype(F32))
    cr, ci = c_re.astype(F32), c_im.astype(F32)

    t_bits = t.bit_length() - 1
    scan_steps = max(n_chunks - 1, 0).bit_length()
    squares = [(a_r, a_i)]
    for _ in range(t_bits + scan_steps):
        squares.append(_cmul(*squares[-1], *squares[-1]))
    taus = jnp.arange(t + 1)
    pr = jnp.ones(a_r.shape + (t + 1,), F32)
    pi = jnp.zeros(a_r.shape + (t + 1,), F32)
    for j in range(t_bits + 1):
        bit = ((taus >> j) & 1) == 1
        pr, pi = _cmul(pr, pi, jnp.where(bit, squares[j][0][..., None], 1.0),
                       jnp.where(bit, squares[j][1][..., None], 0.0))

    groups, p, c = bb_r.shape
    bbt_r, bbt_i = bb_r.transpose(0, 2, 1)[:, :, None, :], bb_i.transpose(0, 2, 1)[:, :, None, :]
    g1r, g1i = _cmul(bbt_r, bbt_i, cr[:, None, :, :], ci[:, None, :, :])
    g1r, g1i = g1r.reshape(groups, c * c, p), g1i.reshape(groups, c * c, p)
    apr, api = pr[..., :t], pi[..., :t]
    rev_r, rev_i = pr[..., t - 1::-1][..., :t], pi[..., t - 1::-1][..., :t]
    bs_r, bs_i = _cmul(rev_r.transpose(0, 2, 1)[:, None], rev_i.transpose(0, 2, 1)[:, None],
                       bb_r.transpose(0, 2, 1)[:, :, None, :], bb_i.transpose(0, 2, 1)[:, :, None, :])
    bs = jnp.concatenate([bs_r, bs_i], axis=-1).reshape(groups, c * t, 2 * p)
    nx_r, nx_i = pr[..., 1:], pi[..., 1:]
    ca_r, ca_i = _cmul(cr.transpose(0, 2, 1)[..., None], ci.transpose(0, 2, 1)[..., None],
                       nx_r[:, :, None, :], nx_i[:, :, None, :])
    cx = jnp.concatenate([ca_r, -ca_i], axis=1).reshape(groups, 2 * p, c * t)
    wc = jnp.stack([jnp.concatenate([squares[t_bits + k][0]] * 2, axis=-1)
                    for k in range(max(scan_steps, 1))], axis=1)
    ws = jnp.stack([jnp.concatenate([-squares[t_bits + k][1], squares[t_bits + k][1]], axis=-1)
                    for k in range(max(scan_steps, 1))], axis=1)
    if scan_steps == 0:
        wc, ws = wc[:, :0], ws[:, :0]
    drow = jnp.repeat(d_skip.astype(F32).reshape(groups, c), t, axis=1)[:, None, :]
    return g1r, g1i, apr, api, bs.astype(BF16), cx.astype(BF16), drow, wc, ws


def _rotary_tables(seq):
    pos = jnp.arange(seq, dtype=F32)
    inv_freq = ROPE_BASE ** (-jnp.arange(0, RET_DK, 2, dtype=F32) / RET_DK)
    ang = pos[:, None] * inv_freq[None, :]
    cos, sin = jnp.cos(ang), jnp.sin(ang)
    return jnp.concatenate([cos, cos], axis=1), jnp.concatenate([-sin, sin], axis=1)


def kernel(x, w_in, lam_re, lam_im, b_re, b_im, c_re, c_im, d_skip, log_step, w_glu, b_glu,
           w_proj_ssm, w_proj_ret, w_out, norm_mix, norm_mlp, w_up, w_down, norm_final):
    batch, seq, d = x.shape
    depth = w_in.shape[0]
    sw = w_glu.shape[1]
    qk = RET_HEADS * RET_DK
    t = min(S5_CHUNK, seq)
    n_chunks = seq // t
    assert seq % t == 0 and seq % min(RET_CHUNK, seq) == 0 and seq % min(TOKEN_TILE, seq) == 0
    cos, sin = _rotary_tables(seq)
    h = x.reshape(batch * seq, d).astype(F32)
    for l in range(depth):
        w = w_in[l].astype(BF16)
        uT, q, k, v, g, ga, gb = _in_proj(
            h, norm_mix[l].reshape(1, d).astype(F32), w[:, :sw].T, w[:, sw:sw + qk],
            w[:, sw + qk:sw + 2 * qk], w[:, sw + 2 * qk:], cos, sin, batch, seq)
        g1r, g1i, apr, api, bs, cx, drow, wc, ws = _s5_tables(
            lam_re[l], lam_im[l], b_re[l], b_im[l], c_re[l], c_im[l], d_skip[l], log_step[l],
            t, n_chunks)
        m = _toeplitz(g1r, g1i, apr, api)
        yT = _s5(uT.reshape(batch, sw, n_chunks, t), m, bs, cx, drow, wc, ws)
        yb = _retention(q, k, v, g, batch, seq)
        h = _mix_ffn(
            h, yT.reshape(batch, sw, seq), yb, ga, gb,
            w_glu[l].T.astype(BF16), b_glu[l].reshape(sw, 1).astype(F32),
            w_proj_ssm[l].astype(BF16), w_proj_ret[l].astype(BF16), w_out[l].astype(BF16),
            norm_mlp[l].reshape(1, d).astype(F32), w_up[l].astype(BF16), w_down[l].astype(BF16),
            norm_final.reshape(1, d).astype(F32), batch, seq, final=(l == depth - 1))
    return h.reshape(batch, seq, d).astype(x.dtype)
```

```python
import functools

import numpy as np
import jax
import jax.numpy as jnp
from jax import lax
from jax.experimental import pallas as pl
from jax.experimental.pallas import tpu as pltpu

F32 = jnp.float32
BF16 = jnp.bfloat16

EPS = 1e-6
SSM_GROUP = 16
SSM_STATE = 64
RET_HEADS = 4
RET_DK = 128
RET_DV = 256
ROPE_BASE = 10000.0

LANES = 128
SUBLANES = 8
S5_CHUNK = LANES
S5_CHUNK_BITS = 7
RET_CHUNK = 256
TOKEN_TILE = 512
VMEM_LIMIT = 56 * 1024 * 1024


def _params(n_axes, vmem=VMEM_LIMIT):
    return pltpu.CompilerParams(
        dimension_semantics=("arbitrary",) * n_axes, vmem_limit_bytes=vmem)


def _whole(shape):
    zeros = (0,) * len(shape)
    return pl.BlockSpec(shape, lambda *_: zeros, pipeline_mode=pl.Buffered(1))


def _dot(a, b):
    return jnp.dot(a, b, preferred_element_type=F32)


def _dot_nt(a, b, precision=None):
    return lax.dot_general(a, b, (((1,), (1,)), ((), ())), precision=precision,
                           preferred_element_type=F32)


def _dot_tn(a, b):
    return lax.dot_general(a, b, (((0,), (0,)), ((), ())), preferred_element_type=F32)


def _rmsnorm(x, w):
    return x * lax.rsqrt(jnp.mean(x * x, axis=-1, keepdims=True) + EPS) * w


def _rot_half(x, c, s):
    return x * c + pltpu.roll(x, LANES // 2, 1) * s


def _in_proj_body(h_ref, nw_ref, wuT_ref, w_ref, cos_ref, sin_ref,
                  u_ref, q_ref, k_ref, v_ref, g_ref, ga_ref, gb_ref):
    z = _rmsnorm(h_ref[...], nw_ref[...]).astype(BF16)
    uT = _dot_nt(wuT_ref[...], z)
    for r in range(uT.shape[0] // SUBLANES):
        for j in range(uT.shape[1] // LANES):
            u_ref[r, j * SUBLANES:(j + 1) * SUBLANES, :] = (
                uT[r * SUBLANES:(r + 1) * SUBLANES, j * LANES:(j + 1) * LANES])
    cos = cos_ref[...]
    sin = sin_ref[...]
    k_scale = RET_DK ** -0.5
    base = uT.shape[0]
    qk = RET_HEADS * RET_DK
    for j in range(RET_HEADS):
        cols = slice(j * RET_DK, (j + 1) * RET_DK)
        wq = w_ref[:, base + j * RET_DK:base + (j + 1) * RET_DK]
        wk = w_ref[:, base + qk + j * RET_DK:base + qk + (j + 1) * RET_DK]
        q_ref[:, cols] = _rot_half(_dot(z, wq), cos, sin).astype(BF16)
        k_ref[:, cols] = (_rot_half(_dot(z, wk), cos, sin) * k_scale).astype(BF16)
    base += 2 * qk
    width = v_ref.shape[1]
    for j, o_ref in enumerate((v_ref, g_ref, ga_ref, gb_ref)):
        o_ref[...] = _dot(z, w_ref[:, base + j * width:base + (j + 1) * width]).astype(BF16)


def _in_proj(h, nw, wuT, w, cos, sin, batch, seq):
    n, d = h.shape
    tm = min(TOKEN_TILE, seq)
    nt = seq // tm
    sw = wuT.shape[0]
    qk = RET_HEADS * RET_DK
    vw = RET_HEADS * RET_DV
    row = lambda wd: pl.BlockSpec((tm, wd), lambda i: (i, 0))
    tab = pl.BlockSpec((tm, LANES), lambda i: (i % nt, 0))
    slabs = sw // SUBLANES
    rows = seq // LANES * SUBLANES
    out_shape = (
        jax.ShapeDtypeStruct((batch, slabs, rows, LANES), F32),
        jax.ShapeDtypeStruct((n, qk), BF16), jax.ShapeDtypeStruct((n, qk), BF16),
        jax.ShapeDtypeStruct((n, vw), BF16), jax.ShapeDtypeStruct((n, vw), BF16),
        jax.ShapeDtypeStruct((n, d), BF16), jax.ShapeDtypeStruct((n, d), BF16),
    )
    return pl.pallas_call(
        _in_proj_body,
        grid=(n // tm,),
        in_specs=[row(d), _whole(nw.shape), _whole(wuT.shape), _whole(w.shape), tab, tab],
        out_specs=(pl.BlockSpec((None, slabs, tm // LANES * SUBLANES, LANES),
                                lambda i: (i // nt, 0, i % nt, 0)),
                   row(qk), row(qk), row(vw), row(vw), row(d), row(d)),
        out_shape=out_shape,
        compiler_params=_params(1),
        name="in_proj",
    )(h, nw, wuT, w, cos, sin)


def _s5_body(u_ref, prow_ref, bx_ref, cx_ref, d_ref, y_ref, taps_ref, m_ref, *, scan_steps):
    batch = u_ref.shape[0]
    n = u_ref.shape[2] // SUBLANES
    t = S5_CHUNK
    half = LANES // 2
    lane = lax.broadcasted_iota(jnp.int32, (1, LANES), 1)
    lo = lane < half
    sgn = jnp.where(lo, -1.0, 1.0)

    def c_form(x):
        return jnp.where(lo, x, pltpu.roll(x, half, 1))

    def s_form(x):
        return jnp.where(lo, -pltpu.roll(x, half, 1), x)

    lr, li = prow_ref[0:1, :], prow_ref[1:2, :]
    step = jnp.exp(prow_ref[2:3, :])
    mag = jnp.exp(lr * step)
    ar, ai = mag * jnp.cos(li * step), mag * jnp.sin(li * step)
    den = lr * lr + li * li
    nr = ar - 1.0
    kr = (nr * lr + ai * li) / den
    ki = (ai * lr - nr * li) / den
    squares = [(ar, ai * sgn)]
    for _ in range(S5_CHUNK_BITS + scan_steps):
        c, s = squares[-1]
        squares.append((c * c - s * s, 2.0 * c * s))

    def powers(expo):
        x = jnp.broadcast_to(jnp.where(lo, 1.0, 0.0), (t, LANES))
        for j in range(S5_CHUNK_BITS):
            x = jnp.where(((expo >> j) & 1) == 1, _rot_half(x, *squares[j]), x)
        return x

    row_id = lax.broadcasted_iota(jnp.int32, (t, LANES), 0)
    a_tau = powers(row_id)
    a_rev = powers(t - 1 - row_id)
    a_next = _rot_half(a_tau, *squares[0])

    bbar = _rot_half(bx_ref[...], kr, ki * sgn)
    bb_c, bb_s = c_form(bbar), s_form(bbar)
    cmat = cx_ref[...]
    cc_c, cc_s = c_form(cmat), s_form(cmat)

    g1 = jnp.concatenate(
        [_rot_half(cmat, bb_c[ci:ci + 1], bb_s[ci:ci + 1]) * (-sgn) for ci in range(SSM_GROUP)], axis=0)
    taps_ref[...] = _dot_nt(g1, a_tau, precision=lax.Precision.HIGHEST)
    bs = jnp.concatenate(
        [_rot_half(a_rev, bb_c[ci:ci + 1], bb_s[ci:ci + 1]) for ci in range(SSM_GROUP)],
        axis=0).astype(BF16)
    cxt = jnp.concatenate(
        [_rot_half(a_next, cc_c[co:co + 1], cc_s[co:co + 1]) * (-sgn) for co in range(SSM_GROUP)],
        axis=0).astype(BF16)

    def channel(ref, b, c):
        return ref.at[b, c // SUBLANES, pl.ds(c % SUBLANES, n, stride=SUBLANES), :]

    u = jnp.concatenate(
        [jnp.concatenate([channel(u_ref, b, c)[...].astype(BF16) for c in range(SSM_GROUP)], axis=1)
         for b in range(batch)], axis=0)
    x = _dot(u, bs)
    rows = lax.broadcasted_iota(jnp.int32, x.shape, 0) % n
    for k in range(scan_steps):
        sh = 1 << k
        prev = jnp.where(rows >= sh, pltpu.roll(x, sh, 0), 0.0)
        x = x + _rot_half(prev, *squares[S5_CHUNK_BITS + k])
    h_prev = jnp.where(rows >= 1, pltpu.roll(x, 1, 0), 0.0).astype(BF16)

    causal = (lax.broadcasted_iota(jnp.int32, (t, t), 1)
              >= lax.broadcasted_iota(jnp.int32, (t, t), 0))
    pair = 2 * t
    for jp in range(SSM_GROUP // 2):
        for ci in range(SSM_GROUP):
            for co in range(2 * jp, 2 * jp + 2):
                tap_row = taps_ref[ci * SSM_GROUP + co:ci * SSM_GROUP + co + 1, :]
                blk = pltpu.roll(jnp.broadcast_to(tap_row, (t, t)), 0, 1, stride=1, stride_axis=0)
                m_ref[ci * t:(ci + 1) * t, co * t:(co + 1) * t] = (
                    jnp.where(causal, blk, 0.0).astype(BF16))
        y2 = (_dot(u, m_ref[:, jp * pair:(jp + 1) * pair])
              + _dot_nt(h_prev, cxt[jp * pair:(jp + 1) * pair, :]))
        for co in range(2 * jp, 2 * jp + 2):
            for b in range(batch):
                skip = d_ref[co:co + 1, :] * channel(u_ref, b, co)[...]
                y = y2[b * n:(b + 1) * n, (co - 2 * jp) * t:(co - 2 * jp + 1) * t] + skip
                channel(y_ref, b, co)[...] = jax.nn.gelu(y)


def _s5(u4, prow, bx, cx, dB):
    batch, slabs, rows, _ = u4.shape
    groups = prow.shape[0]
    n_chunks = rows // SUBLANES
    scan_steps = max(n_chunks - 1, 0).bit_length()
    width = SSM_GROUP * S5_CHUNK
    per = slabs // groups
    grp = lambda *shape: pl.BlockSpec((None,) + shape, lambda g: (g,) + (0,) * len(shape))
    io = pl.BlockSpec((batch, per, rows, LANES), lambda g: (0, g, 0, 0))
    return pl.pallas_call(
        functools.partial(_s5_body, scan_steps=scan_steps),
        grid=(groups,),
        in_specs=[io, grp(3, LANES), grp(SSM_GROUP, LANES), grp(SSM_GROUP, LANES),
                  grp(SSM_GROUP, LANES)],
        out_specs=io,
        out_shape=jax.ShapeDtypeStruct(u4.shape, F32),
        scratch_shapes=[pltpu.VMEM((SSM_GROUP * SSM_GROUP, S5_CHUNK), F32),
                        pltpu.VMEM((width, width), BF16)],
        compiler_params=_params(1),
        name="s5",
    )(u4, prow, bx, cx, dB)


def _retention_body(q_ref, k_ref, v_ref, g_ref, dm_ref, zeta_ref, xi_ref, o_ref, state_ref, *,
                    chunk_decay):
    @pl.when(pl.program_id(1) == 0)
    def _():
        state_ref[...] = jnp.zeros_like(state_ref)

    for hh in range(RET_HEADS):
        qk_cols = slice(hh * RET_DK, (hh + 1) * RET_DK)
        v_cols = slice(hh * RET_DV, (hh + 1) * RET_DV)
        q = q_ref[:, qk_cols]
        k = k_ref[:, qk_cols]
        v = v_ref[:, v_cols]
        state = state_ref[hh]
        scores = _dot_nt(q, k) * dm_ref[hh]
        o = _dot(scores.astype(BF16), v) + xi_ref[hh] * _dot(q, state.astype(BF16))
        kz = (k.astype(F32) * zeta_ref[hh]).astype(BF16)
        state_ref[hh] = chunk_decay[hh] * state + _dot_tn(kz, v)
        o = o * lax.rsqrt(jnp.mean(o * o, axis=-1, keepdims=True) + EPS)
        o_ref[:, v_cols] = (o * jax.nn.silu(g_ref[:, v_cols].astype(F32))).astype(BF16)


def _retention(q, k, v, g, batch, seq):
    n = q.shape[0]
    c = min(RET_CHUNK, seq)
    nc = seq // c
    gamma = 1.0 - 2.0 ** (-5.0 - np.arange(RET_HEADS, dtype=np.float64))
    idx = np.arange(c, dtype=np.float64)
    rel = idx[:, None] - idx[None, :]
    dmask = np.where(rel >= 0, gamma[:, None, None] ** np.maximum(rel, 0.0), 0.0)
    zeta = np.broadcast_to((gamma[:, None] ** (c - 1.0 - idx))[:, :, None], (RET_HEADS, c, RET_DK))
    xi = np.broadcast_to((gamma[:, None] ** (idx + 1.0))[:, :, None], (RET_HEADS, c, RET_DV))
    chunk_decay = tuple(float(x) for x in gamma ** c)
    row = lambda w: pl.BlockSpec((c, w), lambda b, j: (b * nc + j, 0))
    return pl.pallas_call(
        functools.partial(_retention_body, chunk_decay=chunk_decay),
        grid=(batch, nc),
        in_specs=[row(q.shape[1]), row(k.shape[1]), row(v.shape[1]), row(g.shape[1]),
                  _whole(dmask.shape), _whole(zeta.shape), _whole(xi.shape)],
        out_specs=row(v.shape[1]),
        out_shape=jax.ShapeDtypeStruct((n, v.shape[1]), BF16),
        scratch_shapes=[pltpu.VMEM((RET_HEADS, RET_DK, RET_DV), F32)],
        compiler_params=_params(2),
        name="retention",
    )(q, k, v, g, jnp.asarray(dmask, F32), jnp.asarray(zeta, F32), jnp.asarray(xi, F32))


def _mix_ffn_body(h_ref, y_ref, yb_ref, ga_ref, gb_ref, wgT_ref, bg_ref, wpa_ref, wpb_ref,
                  wo_ref, nm_ref, wup_ref, wdn_ref, nf_ref, o_ref, *, final):
    slabs = y_ref.shape[0]
    chunks = y_ref.shape[1] // SUBLANES
    yT = jnp.concatenate(
        [jnp.concatenate([y_ref[r, j * SUBLANES:(j + 1) * SUBLANES, :] for j in range(chunks)], axis=1)
         for r in range(slabs)], axis=0)
    gate = jax.nn.sigmoid(_dot(wgT_ref[...], yT.astype(BF16)) + bg_ref[...])
    yaT = (yT * gate).astype(BF16)
    merged = (jax.nn.sigmoid(ga_ref[...].astype(F32)) * _dot_tn(yaT, wpa_ref[...])
              + jax.nn.sigmoid(gb_ref[...].astype(F32)) * _dot(yb_ref[...], wpb_ref[...]))
    h = h_ref[...] + _dot(merged.astype(BF16), wo_ref[...])
    z = _rmsnorm(h, nm_ref[...]).astype(BF16)
    ff = wup_ref.shape[1]
    step = 1024
    for j in range(0, ff, step):
        a = jnp.square(jnp.maximum(_dot(z, wup_ref[:, j:j + step]), 0.0)).astype(BF16)
        h = h + _dot(a, wdn_ref[j:j + step, :])
    if final:
        h = _rmsnorm(h, nf_ref[...])
    o_ref[...] = h


def _mix_ffn(h, y4, yb, ga, gb, wgT, bg, wpa, wpb, wo, nm, wup, wdn, nf, batch, seq, final):
    n, d = h.shape
    tm = min(TOKEN_TILE, seq)
    nt = seq // tm
    slabs = y4.shape[1]
    row = lambda w: pl.BlockSpec((tm, w), lambda i: (i, 0))
    consts = (wgT, bg, wpa, wpb, wo, nm, wup, wdn, nf)
    return pl.pallas_call(
        functools.partial(_mix_ffn_body, final=final),
        grid=(n // tm,),
        in_specs=[row(d),
                  pl.BlockSpec((None, slabs, tm // LANES * SUBLANES, LANES),
                               lambda i: (i // nt, 0, i % nt, 0)),
                  row(yb.shape[1]), row(d), row(d)] + [_whole(w.shape) for w in consts],
        out_specs=row(d),
        out_shape=jax.ShapeDtypeStruct((n, d), F32),
        compiler_params=_params(1),
        name="mix_ffn",
    )(h, y4, yb, ga, gb, *consts)


def _rotary_tables(seq):
    pos = jnp.arange(seq, dtype=F32)
    inv_freq = ROPE_BASE ** (-jnp.arange(0, RET_DK, 2, dtype=F32) / RET_DK)
    ang = pos[:, None] * inv_freq[None, :]
    cos, sin = jnp.cos(ang), jnp.sin(ang)
    return jnp.concatenate([cos, cos], axis=1), jnp.concatenate([-sin, sin], axis=1)


def _dup(x):
    return jnp.concatenate([x, x], axis=-1)


def kernel(x, w_in, lam_re, lam_im, b_re, b_im, c_re, c_im, d_skip, log_step, w_glu, b_glu,
           w_proj_ssm, w_proj_ret, w_out, norm_mix, norm_mlp, w_up, w_down, norm_final):
    batch, seq, d = x.shape
    depth, groups, p = lam_re.shape
    sw = w_glu.shape[1]
    assert p == SSM_STATE and sw == groups * SSM_GROUP and 2 * p == LANES
    assert seq % S5_CHUNK == 0 and seq % min(RET_CHUNK, seq) == 0 and seq % min(TOKEN_TILE, seq) == 0
    cos, sin = _rotary_tables(seq)
    h = x.reshape(batch * seq, d).astype(F32)
    for l in range(depth):
        w = w_in[l].astype(BF16)
        u4, q, k, v, g, ga, gb = _in_proj(
            h, norm_mix[l].reshape(1, d).astype(F32), w[:, :sw].T, w, cos, sin, batch, seq)
        prow = jnp.stack([_dup(lam_re[l]), _dup(lam_im[l]),
                          jnp.broadcast_to(log_step[l][:, None], (groups, LANES))], axis=1).astype(F32)
        bx = jnp.concatenate([b_re[l], b_im[l]], axis=1).transpose(0, 2, 1).astype(F32)
        cx = jnp.concatenate([c_re[l], c_im[l]], axis=2).astype(F32)
        dB = jnp.broadcast_to(d_skip[l].reshape(groups, SSM_GROUP, 1), (groups, SSM_GROUP, LANES))
        y4 = _s5(u4, prow, bx, cx, dB.astype(F32))
        yb = _retention(q, k, v, g, batch, seq)
        h = _mix_ffn(
            h, y4, yb, ga, gb,
            w_glu[l].T.astype(BF16), b_glu[l].reshape(sw, 1).astype(F32),
            w_proj_ssm[l].astype(BF16), w_proj_ret[l].astype(BF16), w_out[l].astype(BF16),
            norm_mlp[l].reshape(1, d).astype(F32), w_up[l].astype(BF16), w_down[l].astype(BF16),
            norm_final.reshape(1, d).astype(F32), batch, seq, final=(l == depth - 1))
    return h.reshape(batch, seq, d).astype(x.dtype)
```

```python
import functools

import numpy as np
import jax
import jax.numpy as jnp
from jax import lax
from jax.experimental import pallas as pl
from jax.experimental.pallas import tpu as pltpu

F32 = jnp.float32
BF16 = jnp.bfloat16

EPS = 1e-6
SSM_GROUP = 16
SSM_STATE = 64
RET_HEADS = 4
RET_DK = 128
RET_DV = 256
ROPE_BASE = 10000.0

LANES = 128
SUBLANES = 8
S5_CHUNK = LANES
S5_CHUNK_BITS = 7
RET_CHUNK = 256
RET_STEP = 512
TOKEN_TILE = 512
VMEM_LIMIT = 56 * 1024 * 1024


def _params(n_axes, vmem=VMEM_LIMIT):
    return pltpu.CompilerParams(
        dimension_semantics=("arbitrary",) * n_axes, vmem_limit_bytes=vmem)


def _whole(shape):
    zeros = (0,) * len(shape)
    return pl.BlockSpec(shape, lambda *_: zeros, pipeline_mode=pl.Buffered(1))


def _dot(a, b):
    return jnp.dot(a, b, preferred_element_type=F32)


def _dot_nt(a, b, precision=None):
    return lax.dot_general(a, b, (((1,), (1,)), ((), ())), precision=precision,
                           preferred_element_type=F32)


def _dot_tn(a, b):
    return lax.dot_general(a, b, (((0,), (0,)), ((), ())), preferred_element_type=F32)


def _rmsnorm(x, w):
    return x * lax.rsqrt(jnp.mean(x * x, axis=-1, keepdims=True) + EPS) * w


def _rot_half(x, c, s):
    return x * c + pltpu.roll(x, LANES // 2, 1) * s


def _in_proj_body(h_ref, nw_ref, wuT_ref, w_ref, cos_ref, sin_ref,
                  u_ref, q_ref, k_ref, v_ref, g_ref, ga_ref, gb_ref):
    z = _rmsnorm(h_ref[...], nw_ref[...]).astype(BF16)
    uT = _dot_nt(wuT_ref[...], z)
    for r in range(uT.shape[0] // SUBLANES):
        for j in range(uT.shape[1] // LANES):
            u_ref[r, j * SUBLANES:(j + 1) * SUBLANES, :] = (
                uT[r * SUBLANES:(r + 1) * SUBLANES, j * LANES:(j + 1) * LANES])
    cos = cos_ref[...]
    sin = sin_ref[...]
    k_scale = RET_DK ** -0.5
    base = uT.shape[0]
    qk = RET_HEADS * RET_DK
    q_all = _dot(z, w_ref[:, base:base + qk])
    k_all = _dot(z, w_ref[:, base + qk:base + 2 * qk])
    for j in range(RET_HEADS):
        cols = slice(j * RET_DK, (j + 1) * RET_DK)
        q_ref[:, cols] = _rot_half(q_all[:, cols], cos, sin).astype(BF16)
        k_ref[:, cols] = (_rot_half(k_all[:, cols], cos, sin) * k_scale).astype(BF16)
    base += 2 * qk
    width = v_ref.shape[1]
    for j, o_ref in enumerate((v_ref, g_ref, ga_ref, gb_ref)):
        o_ref[...] = _dot(z, w_ref[:, base + j * width:base + (j + 1) * width]).astype(BF16)


def _in_proj(h, nw, wuT, w, cos, sin, batch, seq):
    n, d = h.shape
    tm = min(TOKEN_TILE, seq)
    nt = seq // tm
    sw = wuT.shape[0]
    qk = RET_HEADS * RET_DK
    vw = RET_HEADS * RET_DV
    row = lambda wd: pl.BlockSpec((tm, wd), lambda i: (i, 0))
    tab = pl.BlockSpec((tm, LANES), lambda i: (i % nt, 0))
    slabs = sw // SUBLANES
    rows = seq // LANES * SUBLANES
    out_shape = (
        jax.ShapeDtypeStruct((batch, slabs, rows, LANES), F32),
        jax.ShapeDtypeStruct((n, qk), BF16), jax.ShapeDtypeStruct((n, qk), BF16),
        jax.ShapeDtypeStruct((n, vw), BF16), jax.ShapeDtypeStruct((n, vw), BF16),
        jax.ShapeDtypeStruct((n, d), BF16), jax.ShapeDtypeStruct((n, d), BF16),
    )
    return pl.pallas_call(
        _in_proj_body,
        grid=(n // tm,),
        in_specs=[row(d), _whole(nw.shape), _whole(wuT.shape), _whole(w.shape), tab, tab],
        out_specs=(pl.BlockSpec((None, slabs, tm // LANES * SUBLANES, LANES),
                                lambda i: (i // nt, 0, i % nt, 0)),
                   row(qk), row(qk), row(vw), row(vw), row(d), row(d)),
        out_shape=out_shape,
        compiler_params=_params(1),
        name="in_proj",
    )(h, nw, wuT, w, cos, sin)


def _s5_body(u_ref, prow_ref, bx_ref, cx_ref, d_ref, y_ref, taps_ref, m_ref, *, scan_steps):
    batch = u_ref.shape[0]
    n = u_ref.shape[2] // SUBLANES
    t = S5_CHUNK
    half = LANES // 2
    lane = lax.broadcasted_iota(jnp.int32, (1, LANES), 1)
    lo = lane < half
    sgn = jnp.where(lo, -1.0, 1.0)

    def c_form(x):
        return jnp.where(lo, x, pltpu.roll(x, half, 1))

    def s_form(x):
        return jnp.where(lo, -pltpu.roll(x, half, 1), x)

    lr, li = prow_ref[0:1, :], prow_ref[1:2, :]
    step = jnp.exp(prow_ref[2:3, :])
    mag = jnp.exp(lr * step)
    ar, ai = mag * jnp.cos(li * step), mag * jnp.sin(li * step)
    den = lr * lr + li * li
    nr = ar - 1.0
    kr = (nr * lr + ai * li) / den
    ki = (ai * lr - nr * li) / den
    squares = [(ar, ai * sgn)]
    for _ in range(S5_CHUNK_BITS + scan_steps):
        c, s = squares[-1]
        squares.append((c * c - s * s, 2.0 * c * s))

    def powers(expo):
        x = jnp.broadcast_to(jnp.where(lo, 1.0, 0.0), (t, LANES))
        for j in range(S5_CHUNK_BITS):
            x = jnp.where(((expo >> j) & 1) == 1, _rot_half(x, *squares[j]), x)
        return x

    row_id = lax.broadcasted_iota(jnp.int32, (t, LANES), 0)
    a_tau = powers(row_id)
    a_rev = powers(t - 1 - row_id)
    a_next = _rot_half(a_tau, *squares[0])

    bbar = _rot_half(bx_ref[...], kr, ki * sgn)
    bb_c, bb_s = c_form(bbar), s_form(bbar)
    cmat = cx_ref[...]
    cc_c, cc_s = c_form(cmat), s_form(cmat)

    g1 = jnp.concatenate(
        [_rot_half(cmat, bb_c[ci:ci + 1], bb_s[ci:ci + 1]) * (-sgn) for ci in range(SSM_GROUP)], axis=0)
    taps_ref[...] = _dot_nt(g1, a_tau, precision=lax.Precision.HIGHEST)
    bs = jnp.concatenate(
        [_rot_half(a_rev, bb_c[ci:ci + 1], bb_s[ci:ci + 1]) for ci in range(SSM_GROUP)],
        axis=0).astype(BF16)
    cxt = jnp.concatenate(
        [_rot_half(a_next, cc_c[co:co + 1], cc_s[co:co + 1]) * (-sgn) for co in range(SSM_GROUP)],
        axis=0).astype(BF16)

    def channel(ref, b, c):
        return ref.at[b, c // SUBLANES, pl.ds(c % SUBLANES, n, stride=SUBLANES), :]

    u = jnp.concatenate(
        [jnp.concatenate([channel(u_ref, b, c)[...].astype(BF16) for c in range(SSM_GROUP)], axis=1)
         for b in range(batch)], axis=0)
    x = _dot(u, bs)
    rows = lax.broadcasted_iota(jnp.int32, x.shape, 0) % n
    for k in range(scan_steps):
        sh = 1 << k
        prev = jnp.where(rows >= sh, pltpu.roll(x, sh, 0), 0.0)
        x = x + _rot_half(prev, *squares[S5_CHUNK_BITS + k])
    h_prev = jnp.where(rows >= 1, pltpu.roll(x, 1, 0), 0.0).astype(BF16)

    causal = (lax.broadcasted_iota(jnp.int32, (t, t), 1)
              >= lax.broadcasted_iota(jnp.int32, (t, t), 0))
    pair = 2 * t
    for jp in range(SSM_GROUP // 2):
        for ci in range(SSM_GROUP):
            for co in range(2 * jp, 2 * jp + 2):
                tap_row = taps_ref[ci * SSM_GROUP + co:ci * SSM_GROUP + co + 1, :]
                blk = pltpu.roll(jnp.broadcast_to(tap_row, (t, t)), 0, 1, stride=1, stride_axis=0)
                m_ref[ci * t:(ci + 1) * t, co * t:(co + 1) * t] = (
                    jnp.where(causal, blk, 0.0).astype(BF16))
        y2 = (_dot(u, m_ref[:, jp * pair:(jp + 1) * pair])
              + _dot_nt(h_prev, cxt[jp * pair:(jp + 1) * pair, :]))
        for co in range(2 * jp, 2 * jp + 2):
            for b in range(batch):
                skip = d_ref[co:co + 1, :] * channel(u_ref, b, co)[...]
                y = y2[b * n:(b + 1) * n, (co - 2 * jp) * t:(co - 2 * jp + 1) * t] + skip
                channel(y_ref, b, co)[...] = jax.nn.gelu(y)


def _s5(u4, prow, bx, cx, dB):
    batch, slabs, rows, _ = u4.shape
    groups = prow.shape[0]
    n_chunks = rows // SUBLANES
    scan_steps = max(n_chunks - 1, 0).bit_length()
    width = SSM_GROUP * S5_CHUNK
    per = slabs // groups
    grp = lambda *shape: pl.BlockSpec((None,) + shape, lambda g: (g,) + (0,) * len(shape))
    io = pl.BlockSpec((batch, per, rows, LANES), lambda g: (0, g, 0, 0))
    return pl.pallas_call(
        functools.partial(_s5_body, scan_steps=scan_steps),
        grid=(groups,),
        in_specs=[io, grp(3, LANES), grp(SSM_GROUP, LANES), grp(SSM_GROUP, LANES),
                  grp(SSM_GROUP, LANES)],
        out_specs=io,
        out_shape=jax.ShapeDtypeStruct(u4.shape, F32),
        scratch_shapes=[pltpu.VMEM((SSM_GROUP * SSM_GROUP, S5_CHUNK), F32),
                        pltpu.VMEM((width, width), BF16)],
        compiler_params=_params(1),
        name="s5",
    )(u4, prow, bx, cx, dB)


def _retention_body(q_ref, k_ref, v_ref, g_ref, dm_ref, zeta_ref, xi_ref, o_ref, state_ref, *,
                    chunk_decay):
    @pl.when(pl.program_id(1) == 0)
    def _():
        state_ref[...] = jnp.zeros_like(state_ref)

    c = dm_ref.shape[1]
    for sub in range(q_ref.shape[0] // c):
        rows = slice(sub * c, (sub + 1) * c)
        for hh in range(RET_HEADS):
            qk_cols = slice(hh * RET_DK, (hh + 1) * RET_DK)
            v_cols = slice(hh * RET_DV, (hh + 1) * RET_DV)
            q = q_ref[rows, qk_cols]
            k = k_ref[rows, qk_cols]
            v = v_ref[rows, v_cols]
            state = state_ref[hh]
            scores = (_dot_nt(q, k) * dm_ref[hh]).astype(BF16)
            q_xi = (q.astype(F32) * xi_ref[hh]).astype(BF16)
            o = _dot(jnp.concatenate([scores, q_xi], axis=1),
                     jnp.concatenate([v, state.astype(BF16)], axis=0))
            kz = (k.astype(F32) * zeta_ref[hh]).astype(BF16)
            state_ref[hh] = chunk_decay[hh] * state + _dot_tn(kz, v)
            o = o * lax.rsqrt(jnp.mean(o * o, axis=-1, keepdims=True) + EPS)
            o_ref[rows, v_cols] = (o * jax.nn.silu(g_ref[rows, v_cols].astype(F32))).astype(BF16)


def _retention(q, k, v, g, batch, seq):
    n = q.shape[0]
    c = min(RET_CHUNK, seq)
    step = min(RET_STEP, seq)
    nc = seq // step
    gamma = 1.0 - 2.0 ** (-5.0 - np.arange(RET_HEADS, dtype=np.float64))
    idx = np.arange(c, dtype=np.float64)
    rel = idx[:, None] - idx[None, :]
    dmask = np.where(rel >= 0, gamma[:, None, None] ** np.maximum(rel, 0.0), 0.0)
    zeta = np.broadcast_to((gamma[:, None] ** (c - 1.0 - idx))[:, :, None], (RET_HEADS, c, RET_DK))
    xi = np.broadcast_to((gamma[:, None] ** (idx + 1.0))[:, :, None], (RET_HEADS, c, RET_DK))
    chunk_decay = tuple(float(x) for x in gamma ** c)
    row = lambda w: pl.BlockSpec((step, w), lambda b, j: (b * nc + j, 0))
    return pl.pallas_call(
        functools.partial(_retention_body, chunk_decay=chunk_decay),
        grid=(batch, nc),
        in_specs=[row(q.shape[1]), row(k.shape[1]), row(v.shape[1]), row(g.shape[1]),
                  _whole(dmask.shape), _whole(zeta.shape), _whole(xi.shape)],
        out_specs=row(v.shape[1]),
        out_shape=jax.ShapeDtypeStruct((n, v.shape[1]), BF16),
        scratch_shapes=[pltpu.VMEM((RET_HEADS, RET_DK, RET_DV), F32)],
        compiler_params=_params(2),
        name="retention",
    )(q, k, v, g, jnp.asarray(dmask, F32), jnp.asarray(zeta, F32), jnp.asarray(xi, F32))


def _mix_ffn_body(h_ref, y_ref, yb_ref, ga_ref, gb_ref, wgT_ref, bg_ref, wpa_ref, wpb_ref,
                  wo_ref, nm_ref, wup_ref, wdn_ref, nf_ref, o_ref, *, final):
    slabs = y_ref.shape[0]
    chunks = y_ref.shape[1] // SUBLANES
    yT = jnp.concatenate(
        [jnp.concatenate([y_ref[r, j * SUBLANES:(j + 1) * SUBLANES, :] for j in range(chunks)], axis=1)
         for r in range(slabs)], axis=0)
    gate = jax.nn.sigmoid(_dot(wgT_ref[...], yT.astype(BF16)) + bg_ref[...])
    yaT = (yT * gate).astype(BF16)
    merged = (jax.nn.sigmoid(ga_ref[...].astype(F32)) * _dot_tn(yaT, wpa_ref[...])
              + jax.nn.sigmoid(gb_ref[...].astype(F32)) * _dot(yb_ref[...], wpb_ref[...]))
    h = h_ref[...] + _dot(merged.astype(BF16), wo_ref[...])
    z = _rmsnorm(h, nm_ref[...]).astype(BF16)
    ff = wup_ref.shape[1]
    step = 1024
    for j in range(0, ff, step):
        a = jnp.square(jnp.maximum(_dot(z, wup_ref[:, j:j + step]), 0.0)).astype(BF16)
        h = h + _dot(a, wdn_ref[j:j + step, :])
    if final:
        h = _rmsnorm(h, nf_ref[...])
    o_ref[...] = h


def _mix_ffn(h, y4, yb, ga, gb, wgT, bg, wpa, wpb, wo, nm, wup, wdn, nf, batch, seq, final):
    n, d = h.shape
    tm = min(TOKEN_TILE, seq)
    nt = seq // tm
    slabs = y4.shape[1]
    row = lambda w: pl.BlockSpec((tm, w), lambda i: (i, 0))
    consts = (wgT, bg, wpa, wpb, wo, nm, wup, wdn, nf)
    return pl.pallas_call(
        functools.partial(_mix_ffn_body, final=final),
        grid=(n // tm,),
        in_specs=[row(d),
                  pl.BlockSpec((None, slabs, tm // LANES * SUBLANES, LANES),
                               lambda i: (i // nt, 0, i % nt, 0)),
                  row(yb.shape[1]), row(d), row(d)] + [_whole(w.shape) for w in consts],
        out_specs=row(d),
        out_shape=jax.ShapeDtypeStruct((n, d), F32),
        compiler_params=_params(1),
        name="mix_ffn",
    )(h, y4, yb, ga, gb, *consts)


def _rotary_tables(seq):
    pos = jnp.arange(seq, dtype=F32)
    inv_freq = ROPE_BASE ** (-jnp.arange(0, RET_DK, 2, dtype=F32) / RET_DK)
    ang = pos[:, None] * inv_freq[None, :]
    cos, sin = jnp.cos(ang), jnp.sin(ang)
    return jnp.concatenate([cos, cos], axis=1), jnp.concatenate([-sin, sin], axis=1)


def _dup(x):
    return jnp.concatenate([x, x], axis=-1)


def kernel(x, w_in, lam_re, lam_im, b_re, b_im, c_re, c_im, d_skip, log_step, w_glu, b_glu,
           w_proj_ssm, w_proj_ret, w_out, norm_mix, norm_mlp, w_up, w_down, norm_final):
    batch, seq, d = x.shape
    depth, groups, p = lam_re.shape
    sw = w_glu.shape[1]
    assert p == SSM_STATE and sw == groups * SSM_GROUP and 2 * p == LANES
    assert seq % S5_CHUNK == 0 and seq % min(RET_STEP, seq) == 0 and seq % min(TOKEN_TILE, seq) == 0
    assert min(RET_STEP, seq) % min(RET_CHUNK, seq) == 0
    cos, sin = _rotary_tables(seq)
    h = x.reshape(batch * seq, d).astype(F32)
    for l in range(depth):
        w = w_in[l].astype(BF16)
        u4, q, k, v, g, ga, gb = _in_proj(
            h, norm_mix[l].reshape(1, d).astype(F32), w[:, :sw].T, w, cos, sin, batch, seq)
        prow = jnp.stack([_dup(lam_re[l]), _dup(lam_im[l]),
                          jnp.broadcast_to(log_step[l][:, None], (groups, LANES))], axis=1).astype(F32)
        bx = jnp.concatenate([b_re[l], b_im[l]], axis=1).transpose(0, 2, 1).astype(F32)
        cx = jnp.concatenate([c_re[l], c_im[l]], axis=2).astype(F32)
        dB = jnp.broadcast_to(d_skip[l].reshape(groups, SSM_GROUP, 1), (groups, SSM_GROUP, LANES))
        y4 = _s5(u4, prow, bx, cx, dB.astype(F32))
        yb = _retention(q, k, v, g, batch, seq)
        h = _mix_ffn(
            h, y4, yb, ga, gb,
            w_glu[l].T.astype(BF16), b_glu[l].reshape(sw, 1).astype(F32),
            w_proj_ssm[l].astype(BF16), w_proj_ret[l].astype(BF16), w_out[l].astype(BF16),
            norm_mlp[l].reshape(1, d).astype(F32), w_up[l].astype(BF16), w_down[l].astype(BF16),
            norm_final.reshape(1, d).astype(F32), batch, seq, final=(l == depth - 1))
    return h.reshape(batch, seq, d).astype(x.dtype)
```

```python
import functools

import numpy as np
import jax
import jax.numpy as jnp
from jax import lax
from jax.experimental import pallas as pl
from jax.experimental.pallas import tpu as pltpu

F32 = jnp.float32
BF16 = jnp.bfloat16

EPS = 1e-6
SSM_GROUP = 16
SSM_STATE = 64
RET_HEADS = 4
RET_DK = 128
RET_DV = 256
ROPE_BASE = 10000.0

LANES = 128
SUBLANES = 8
S5_CHUNK = LANES
S5_CHUNK_BITS = 7
RET_CHUNK = 256
TOKEN_TILE = 512
VMEM_LIMIT = 56 * 1024 * 1024


def _params(n_axes, vmem=VMEM_LIMIT):
    return pltpu.CompilerParams(
        dimension_semantics=("arbitrary",) * n_axes, vmem_limit_bytes=vmem)


def _whole(shape):
    zeros = (0,) * len(shape)
    return pl.BlockSpec(shape, lambda *_: zeros, pipeline_mode=pl.Buffered(1))


def _dot(a, b):
    return jnp.dot(a, b, preferred_element_type=F32)


def _dot_nt(a, b, precision=None):
    return lax.dot_general(a, b, (((1,), (1,)), ((), ())), precision=precision,
                           preferred_element_type=F32)


def _dot_tn(a, b):
    return lax.dot_general(a, b, (((0,), (0,)), ((), ())), preferred_element_type=F32)


def _rmsnorm(x, w):
    return x * lax.rsqrt(jnp.mean(x * x, axis=-1, keepdims=True) + EPS) * w


def _rot_half(x, c, s):
    return x * c + pltpu.roll(x, LANES // 2, 1) * s


def _in_proj_body(h_ref, nw_ref, wuT_ref, w_ref, cos_ref, sin_ref, xi_ref, ks_ref,
                  u_ref, o_ref, ga_ref, gb_ref, state_ref, *, chunk_decay, tiles_per_seq):
    @pl.when(pl.program_id(0) % tiles_per_seq == 0)
    def _():
        state_ref[...] = jnp.zeros_like(state_ref)

    z = _rmsnorm(h_ref[...], nw_ref[...]).astype(BF16)
    uT = _dot_nt(wuT_ref[...], z)
    for r in range(uT.shape[0] // SUBLANES):
        for j in range(uT.shape[1] // LANES):
            u_ref[r, j * SUBLANES:(j + 1) * SUBLANES, :] = (
                uT[r * SUBLANES:(r + 1) * SUBLANES, j * LANES:(j + 1) * LANES])
    cos = cos_ref[...]
    sin = sin_ref[...]
    base = uT.shape[0]
    qk = RET_HEADS * RET_DK
    vw = RET_HEADS * RET_DV
    d = ga_ref.shape[1]
    q_all = _dot(z, w_ref[:, base:base + qk])
    k_all = _dot(z, w_ref[:, base + qk:base + 2 * qk])
    base += 2 * qk
    v_all = _dot(z, w_ref[:, base:base + vw]).astype(BF16)
    g_all = _dot(z, w_ref[:, base + vw:base + 2 * vw])
    base += 2 * vw
    gate_dots = [(ga_ref, base), (gb_ref, base + d)]

    c = xi_ref.shape[1]
    tm = z.shape[0]
    causal = (lax.broadcasted_iota(jnp.int32, (c, c), 0)
              >= lax.broadcasted_iota(jnp.int32, (c, c), 1))
    for sub in range(tm // c):
        rows = slice(sub * c, (sub + 1) * c)
        for hh in range(RET_HEADS):
            qk_cols = slice(hh * RET_DK, (hh + 1) * RET_DK)
            v_cols = slice(hh * RET_DV, (hh + 1) * RET_DV)
            q = (_rot_half(q_all[rows, qk_cols], cos[rows], sin[rows]) * xi_ref[hh]).astype(BF16)
            k = (_rot_half(k_all[rows, qk_cols], cos[rows], sin[rows]) * ks_ref[hh]).astype(BF16)
            v = v_all[rows, v_cols]
            state = state_ref[hh]
            scores = jnp.where(causal, _dot_nt(q, k), 0.0).astype(BF16)
            o = _dot(jnp.concatenate([scores, q], axis=1),
                     jnp.concatenate([v, state.astype(BF16)], axis=0))
            state_ref[hh] = chunk_decay[hh] * (state + _dot_tn(k, v))
            o = o * lax.rsqrt(jnp.mean(o * o, axis=-1, keepdims=True) + EPS)
            o_ref[rows, v_cols] = (o * jax.nn.silu(g_all[rows, v_cols])).astype(BF16)
        if gate_dots:
            gate_ref, col = gate_dots.pop(0)
            gate_ref[...] = _dot(z, w_ref[:, col:col + d]).astype(BF16)
    for gate_ref, col in gate_dots:
        gate_ref[...] = _dot(z, w_ref[:, col:col + d]).astype(BF16)


def _in_proj(h, nw, wuT, w, cos, sin, batch, seq):
    n, d = h.shape
    tm = min(TOKEN_TILE, seq)
    nt = seq // tm
    sw = wuT.shape[0]
    vw = RET_HEADS * RET_DV
    c = min(RET_CHUNK, tm)
    gamma = 1.0 - 2.0 ** (-5.0 - np.arange(RET_HEADS, dtype=np.float64))
    idx = np.arange(c, dtype=np.float64)
    xi = np.broadcast_to((gamma[:, None] ** (idx + 1.0))[:, :, None], (RET_HEADS, c, RET_DK))
    ks = np.broadcast_to((RET_DK ** -0.5 * gamma[:, None] ** (-1.0 - idx))[:, :, None],
                         (RET_HEADS, c, RET_DK))
    chunk_decay = tuple(float(x) for x in gamma ** c)
    row = lambda wd: pl.BlockSpec((tm, wd), lambda i: (i, 0))
    tab = pl.BlockSpec((tm, LANES), lambda i: (i % nt, 0))
    slabs = sw // SUBLANES
    rows = seq // LANES * SUBLANES
    out_shape = (
        jax.ShapeDtypeStruct((batch, slabs, rows, LANES), F32),
        jax.ShapeDtypeStruct((n, vw), BF16),
        jax.ShapeDtypeStruct((n, d), BF16), jax.ShapeDtypeStruct((n, d), BF16),
    )
    return pl.pallas_call(
        functools.partial(_in_proj_body, chunk_decay=chunk_decay, tiles_per_seq=nt),
        grid=(n // tm,),
        in_specs=[row(d), _whole(nw.shape), _whole(wuT.shape), _whole(w.shape), tab, tab,
                  _whole(xi.shape), _whole(ks.shape)],
        out_specs=(pl.BlockSpec((None, slabs, tm // LANES * SUBLANES, LANES),
                                lambda i: (i // nt, 0, i % nt, 0)),
                   row(vw), row(d), row(d)),
        out_shape=out_shape,
        scratch_shapes=[pltpu.VMEM((RET_HEADS, RET_DK, RET_DV), F32)],
        compiler_params=_params(1),
        name="in_proj",
    )(h, nw, wuT, w, cos, sin, jnp.asarray(xi, F32), jnp.asarray(ks, F32))


def _s5_body(u_ref, prow_ref, bx_ref, cx_ref, d_ref, y_ref, taps_ref, m_ref, *, scan_steps):
    batch = u_ref.shape[0]
    n = u_ref.shape[2] // SUBLANES
    t = S5_CHUNK
    half = LANES // 2
    lane = lax.broadcasted_iota(jnp.int32, (1, LANES), 1)
    lo = lane < half
    sgn = jnp.where(lo, -1.0, 1.0)

    def c_form(x):
        return jnp.where(lo, x, pltpu.roll(x, half, 1))

    def s_form(x):
        return jnp.where(lo, -pltpu.roll(x, half, 1), x)

    lr, li = prow_ref[0:1, :], prow_ref[1:2, :]
    step = jnp.exp(prow_ref[2:3, :])
    mag = jnp.exp(lr * step)
    ar, ai = mag * jnp.cos(li * step), mag * jnp.sin(li * step)
    den = lr * lr + li * li
    nr = ar - 1.0
    kr = (nr * lr + ai * li) / den
    ki = (ai * lr - nr * li) / den
    squares = [(ar, ai * sgn)]
    for _ in range(S5_CHUNK_BITS + scan_steps):
        c, s = squares[-1]
        squares.append((c * c - s * s, 2.0 * c * s))

    def powers(expo):
        x = jnp.broadcast_to(jnp.where(lo, 1.0, 0.0), (t, LANES))
        for j in range(S5_CHUNK_BITS):
            x = jnp.where(((expo >> j) & 1) == 1, _rot_half(x, *squares[j]), x)
        return x

    row_id = lax.broadcasted_iota(jnp.int32, (t, LANES), 0)
    a_tau = powers(row_id)
    a_rev = powers(t - 1 - row_id)
    a_next = _rot_half(a_tau, *squares[0])

    bbar = _rot_half(bx_ref[...], kr, ki * sgn)
    bb_c, bb_s = c_form(bbar), s_form(bbar)
    cmat = cx_ref[...]
    cc_c, cc_s = c_form(cmat), s_form(cmat)

    g1 = jnp.concatenate(
        [_rot_half(cmat, bb_c[ci:ci + 1], bb_s[ci:ci + 1]) * (-sgn) for ci in range(SSM_GROUP)], axis=0)
    taps_ref[...] = _dot_nt(g1, a_tau, precision=lax.Precision.HIGHEST)
    bs = jnp.concatenate(
        [_rot_half(a_rev, bb_c[ci:ci + 1], bb_s[ci:ci + 1]) for ci in range(SSM_GROUP)],
        axis=0).astype(BF16)
    cxt = jnp.concatenate(
        [_rot_half(a_next, cc_c[co:co + 1], cc_s[co:co + 1]) * (-sgn) for co in range(SSM_GROUP)],
        axis=0).astype(BF16)

    def channel(ref, b, c):
        return ref.at[b, c // SUBLANES, pl.ds(c % SUBLANES, n, stride=SUBLANES), :]

    u = jnp.concatenate(
        [jnp.concatenate([channel(u_ref, b, c)[...].astype(BF16) for c in range(SSM_GROUP)], axis=1)
         for b in range(batch)], axis=0)
    x = _dot(u, bs)
    rows = lax.broadcasted_iota(jnp.int32, x.shape, 0) % n
    for k in range(scan_steps):
        sh = 1 << k
        prev = jnp.where(rows >= sh, pltpu.roll(x, sh, 0), 0.0)
        x = x + _rot_half(prev, *squares[S5_CHUNK_BITS + k])
    h_prev = jnp.where(rows >= 1, pltpu.roll(x, 1, 0), 0.0).astype(BF16)

    causal = (lax.broadcasted_iota(jnp.int32, (t, t), 1)
              >= lax.broadcasted_iota(jnp.int32, (t, t), 0))
    pair = 2 * t
    for jp in range(SSM_GROUP // 2):
        for ci in range(SSM_GROUP):
            for co in range(2 * jp, 2 * jp + 2):
                tap_row = taps_ref[ci * SSM_GROUP + co:ci * SSM_GROUP + co + 1, :]
                blk = pltpu.roll(jnp.broadcast_to(tap_row, (t, t)), 0, 1, stride=1, stride_axis=0)
                m_ref[ci * t:(ci + 1) * t, co * t:(co + 1) * t] = (
                    jnp.where(causal, blk, 0.0).astype(BF16))
        y2 = (_dot(u, m_ref[:, jp * pair:(jp + 1) * pair])
              + _dot_nt(h_prev, cxt[jp * pair:(jp + 1) * pair, :]))
        for co in range(2 * jp, 2 * jp + 2):
            for b in range(batch):
                skip = d_ref[co:co + 1, :] * channel(u_ref, b, co)[...]
                y = y2[b * n:(b + 1) * n, (co - 2 * jp) * t:(co - 2 * jp + 1) * t] + skip
                channel(y_ref, b, co)[...] = jax.nn.gelu(y)


def _s5(u4, prow, bx, cx, dB):
    batch, slabs, rows, _ = u4.shape
    groups = prow.shape[0]
    n_chunks = rows // SUBLANES
    scan_steps = max(n_chunks - 1, 0).bit_length()
    width = SSM_GROUP * S5_CHUNK
    per = slabs // groups
    grp = lambda *shape: pl.BlockSpec((None,) + shape, lambda g: (g,) + (0,) * len(shape))
    io = pl.BlockSpec((batch, per, rows, LANES), lambda g: (0, g, 0, 0))
    return pl.pallas_call(
        functools.partial(_s5_body, scan_steps=scan_steps),
        grid=(groups,),
        in_specs=[io, grp(3, LANES), grp(SSM_GROUP, LANES), grp(SSM_GROUP, LANES),
                  grp(SSM_GROUP, LANES)],
        out_specs=io,
        out_shape=jax.ShapeDtypeStruct(u4.shape, F32),
        scratch_shapes=[pltpu.VMEM((SSM_GROUP * SSM_GROUP, S5_CHUNK), F32),
                        pltpu.VMEM((width, width), BF16)],
        compiler_params=_params(1),
        name="s5",
    )(u4, prow, bx, cx, dB)


def _mix_ffn_body(h_ref, y_ref, yb_ref, ga_ref, gb_ref, wgT_ref, bg_ref, wpa_ref, wpb_ref,
                  wo_ref, nm_ref, wup_ref, wdn_ref, nf_ref, o_ref, *, final):
    slabs = y_ref.shape[0]
    chunks = y_ref.shape[1] // SUBLANES
    yT = jnp.concatenate(
        [jnp.concatenate([y_ref[r, j * SUBLANES:(j + 1) * SUBLANES, :] for j in range(chunks)], axis=1)
         for r in range(slabs)], axis=0)
    gate = jax.nn.sigmoid(_dot(wgT_ref[...], yT.astype(BF16)) + bg_ref[...])
    yaT = (yT * gate).astype(BF16)
    merged = (jax.nn.sigmoid(ga_ref[...].astype(F32)) * _dot_tn(yaT, wpa_ref[...])
              + jax.nn.sigmoid(gb_ref[...].astype(F32)) * _dot(yb_ref[...], wpb_ref[...]))
    h = h_ref[...] + _dot(merged.astype(BF16), wo_ref[...])
    z = _rmsnorm(h, nm_ref[...]).astype(BF16)
    ff = wup_ref.shape[1]
    step = 1024
    for j in range(0, ff, step):
        a = jnp.square(jnp.maximum(_dot(z, wup_ref[:, j:j + step]), 0.0)).astype(BF16)
        h = h + _dot(a, wdn_ref[j:j + step, :])
    if final:
        h = _rmsnorm(h, nf_ref[...])
    o_ref[...] = h


def _mix_ffn(h, y4, yb, ga, gb, wgT, bg, wpa, wpb, wo, nm, wup, wdn, nf, batch, seq, final):
    n, d = h.shape
    tm = min(TOKEN_TILE, seq)
    nt = seq // tm
    slabs = y4.shape[1]
    row = lambda w: pl.BlockSpec((tm, w), lambda i: (i, 0))
    consts = (wgT, bg, wpa, wpb, wo, nm, wup, wdn, nf)
    return pl.pallas_call(
        functools.partial(_mix_ffn_body, final=final),
        grid=(n // tm,),
        in_specs=[row(d),
                  pl.BlockSpec((None, slabs, tm // LANES * SUBLANES, LANES),
                               lambda i: (i // nt, 0, i % nt, 0)),
                  row(yb.shape[1]), row(d), row(d)] + [_whole(w.shape) for w in consts],
        out_specs=row(d),
        out_shape=jax.ShapeDtypeStruct((n, d), F32),
        compiler_params=_params(1),
        name="mix_ffn",
    )(h, y4, yb, ga, gb, *consts)


def _rotary_tables(seq):
    pos = jnp.arange(seq, dtype=F32)
    inv_freq = ROPE_BASE ** (-jnp.arange(0, RET_DK, 2, dtype=F32) / RET_DK)
    ang = pos[:, None] * inv_freq[None, :]
    cos, sin = jnp.cos(ang), jnp.sin(ang)
    return jnp.concatenate([cos, cos], axis=1), jnp.concatenate([-sin, sin], axis=1)


def _dup(x):
    return jnp.concatenate([x, x], axis=-1)


def kernel(x, w_in, lam_re, lam_im, b_re, b_im, c_re, c_im, d_skip, log_step, w_glu, b_glu,
           w_proj_ssm, w_proj_ret, w_out, norm_mix, norm_mlp, w_up, w_down, norm_final):
    batch, seq, d = x.shape
    depth, groups, p = lam_re.shape
    sw = w_glu.shape[1]
    assert p == SSM_STATE and sw == groups * SSM_GROUP and 2 * p == LANES
    assert seq % S5_CHUNK == 0 and seq % min(TOKEN_TILE, seq) == 0
    assert min(TOKEN_TILE, seq) % min(RET_CHUNK, seq) == 0
    cos, sin = _rotary_tables(seq)
    h = x.reshape(batch * seq, d).astype(F32)
    for l in range(depth):
        w = w_in[l].astype(BF16)
        u4, yb, ga, gb = _in_proj(
            h, norm_mix[l].reshape(1, d).astype(F32), w[:, :sw].T, w, cos, sin, batch, seq)
        prow = jnp.stack([_dup(lam_re[l]), _dup(lam_im[l]),
                          jnp.broadcast_to(log_step[l][:, None], (groups, LANES))], axis=1).astype(F32)
        bx = jnp.concatenate([b_re[l], b_im[l]], axis=1).transpose(0, 2, 1).astype(F32)
        cx = jnp.concatenate([c_re[l], c_im[l]], axis=2).astype(F32)
        dB = jnp.broadcast_to(d_skip[l].reshape(groups, SSM_GROUP, 1), (groups, SSM_GROUP, LANES))
        y4 = _s5(u4, prow, bx, cx, dB.astype(F32))
        h = _mix_ffn(
            h, y4, yb, ga, gb,
            w_glu[l].T.astype(BF16), b_glu[l].reshape(sw, 1).astype(F32),
            w_proj_ssm[l].astype(BF16), w_proj_ret[l].astype(BF16), w_out[l].astype(BF16),
            norm_mlp[l].reshape(1, d).astype(F32), w_up[l].astype(BF16), w_down[l].astype(BF16),
            norm_final.reshape(1, d).astype(F32), batch, seq, final=(l == depth - 1))
    return h.reshape(batch, seq, d).astype(x.dtype)
```

```python
import functools

import numpy as np
import jax
import jax.numpy as jnp
from jax import lax
from jax.experimental import pallas as pl
from jax.experimental.pallas import tpu as pltpu

F32 = jnp.float32
BF16 = jnp.bfloat16

EPS = 1e-6
SSM_GROUP = 16
SSM_STATE = 64
RET_HEADS = 4
RET_DK = 128
RET_DV = 256
ROPE_BASE = 10000.0

LANES = 128
SUBLANES = 8
S5_CHUNK = LANES
S5_CHUNK_BITS = 7
S5_GROUPS_PER_STEP = 2
RET_CHUNK = 256
TOKEN_TILE = 512
VMEM_LIMIT = 56 * 1024 * 1024


def _params(n_axes, vmem=VMEM_LIMIT):
    return pltpu.CompilerParams(
        dimension_semantics=("arbitrary",) * n_axes, vmem_limit_bytes=vmem)


def _whole(shape):
    zeros = (0,) * len(shape)
    return pl.BlockSpec(shape, lambda *_: zeros, pipeline_mode=pl.Buffered(1))


def _dot(a, b):
    return jnp.dot(a, b, preferred_element_type=F32)


def _dot_nt(a, b, precision=None):
    return lax.dot_general(a, b, (((1,), (1,)), ((), ())), precision=precision,
                           preferred_element_type=F32)


def _dot_tn(a, b):
    return lax.dot_general(a, b, (((0,), (0,)), ((), ())), preferred_element_type=F32)


def _rmsnorm(x, w):
    return x * lax.rsqrt(jnp.mean(x * x, axis=-1, keepdims=True) + EPS) * w


def _rot_half(x, c, s):
    return x * c + pltpu.roll(x, LANES // 2, 1) * s


def _in_proj_body(h_ref, nw_ref, wuT_ref, w_ref, cos_ref, sin_ref, cos0_ref, sin0_ref, xi_ref, ks_ref,
                  u_ref, o_ref, ga_ref, gb_ref, state_ref, *, chunk_decay, tiles_per_seq):
    @pl.when(pl.program_id(0) % tiles_per_seq == 0)
    def _():
        state_ref[...] = jnp.zeros_like(state_ref)

    z = _rmsnorm(h_ref[...], nw_ref[...]).astype(BF16)
    uT = _dot_nt(wuT_ref[...], z)
    for r in range(uT.shape[0] // SUBLANES):
        for j in range(uT.shape[1] // LANES):
            u_ref[r, j * SUBLANES:(j + 1) * SUBLANES, :] = (
                uT[r * SUBLANES:(r + 1) * SUBLANES, j * LANES:(j + 1) * LANES])
    cos = cos_ref[...] * cos0_ref[...] - sin_ref[...] * sin0_ref[...]
    sin = sin_ref[...] * cos0_ref[...] + cos_ref[...] * sin0_ref[...]
    base = uT.shape[0]
    qk = RET_HEADS * RET_DK
    vw = RET_HEADS * RET_DV
    d = ga_ref.shape[1]
    q_all = _dot(z, w_ref[:, base:base + qk])
    k_all = _dot(z, w_ref[:, base + qk:base + 2 * qk])
    base += 2 * qk
    v_all = _dot(z, w_ref[:, base:base + vw]).astype(BF16)
    g_all = _dot(z, w_ref[:, base + vw:base + 2 * vw])
    base += 2 * vw
    gate_dots = [(ga_ref, base), (gb_ref, base + d)]

    c = xi_ref.shape[1]
    tm = z.shape[0]
    causal = (lax.broadcasted_iota(jnp.int32, (c, c), 0)
              >= lax.broadcasted_iota(jnp.int32, (c, c), 1))
    for sub in range(tm // c):
        rows = slice(sub * c, (sub + 1) * c)
        for hh in range(RET_HEADS):
            qk_cols = slice(hh * RET_DK, (hh + 1) * RET_DK)
            v_cols = slice(hh * RET_DV, (hh + 1) * RET_DV)
            q = (_rot_half(q_all[rows, qk_cols], cos[rows], sin[rows]) * xi_ref[hh]).astype(BF16)
            k = (_rot_half(k_all[rows, qk_cols], cos[rows], sin[rows]) * ks_ref[hh]).astype(BF16)
            v = v_all[rows, v_cols]
            state = state_ref[hh]
            scores = jnp.where(causal, _dot_nt(q, k), 0.0).astype(BF16)
            o = _dot(jnp.concatenate([scores, q], axis=1),
                     jnp.concatenate([v, state.astype(BF16)], axis=0))
            state_ref[hh] = chunk_decay[hh] * (state + _dot_tn(k, v))
            o = o * lax.rsqrt(jnp.mean(o * o, axis=-1, keepdims=True) + EPS)
            o_ref[rows, v_cols] = (o * jax.nn.silu(g_all[rows, v_cols])).astype(BF16)
        if gate_dots:
            gate_ref, col = gate_dots.pop(0)
            gate_ref[...] = _dot(z, w_ref[:, col:col + d]).astype(BF16)
    for gate_ref, col in gate_dots:
        gate_ref[...] = _dot(z, w_ref[:, col:col + d]).astype(BF16)


def _in_proj(h, nw, wuT, w, batch, seq):
    n, d = h.shape
    tm = min(TOKEN_TILE, seq)
    nt = seq // tm
    cos, sin = _rotary_tables(jnp.arange(tm, dtype=F32))
    cos0, sin0 = _rotary_tables(jnp.arange(nt, dtype=F32) * tm)
    cos0, sin0 = cos0.reshape(nt, 1, LANES), sin0.reshape(nt, 1, LANES)
    sw = wuT.shape[0]
    vw = RET_HEADS * RET_DV
    c = min(RET_CHUNK, tm)
    gamma = 1.0 - 2.0 ** (-5.0 - np.arange(RET_HEADS, dtype=np.float64))
    idx = np.arange(c, dtype=np.float64)
    xi = np.broadcast_to((gamma[:, None] ** (idx + 1.0))[:, :, None], (RET_HEADS, c, RET_DK))
    ks = np.broadcast_to((RET_DK ** -0.5 * gamma[:, None] ** (-1.0 - idx))[:, :, None],
                         (RET_HEADS, c, RET_DK))
    chunk_decay = tuple(float(x) for x in gamma ** c)
    row = lambda wd: pl.BlockSpec((tm, wd), lambda i: (i, 0))
    tab0 = pl.BlockSpec((None, 1, LANES), lambda i: (i % nt, 0, 0))
    slabs = sw // SUBLANES
    rows = seq // LANES * SUBLANES
    out_shape = (
        jax.ShapeDtypeStruct((batch, slabs, rows, LANES), F32),
        jax.ShapeDtypeStruct((n, vw), BF16),
        jax.ShapeDtypeStruct((n, d), BF16), jax.ShapeDtypeStruct((n, d), BF16),
    )
    return pl.pallas_call(
        functools.partial(_in_proj_body, chunk_decay=chunk_decay, tiles_per_seq=nt),
        grid=(n // tm,),
        in_specs=[row(d), _whole(nw.shape), _whole(wuT.shape), _whole(w.shape),
                  _whole(cos.shape), _whole(sin.shape), tab0, tab0,
                  _whole(xi.shape), _whole(ks.shape)],
        out_specs=(pl.BlockSpec((None, slabs, tm // LANES * SUBLANES, LANES),
                                lambda i: (i // nt, 0, i % nt, 0)),
                   row(vw), row(d), row(d)),
        out_shape=out_shape,
        scratch_shapes=[pltpu.VMEM((RET_HEADS, RET_DK, RET_DV), F32)],
        compiler_params=_params(1),
        name="in_proj",
    )(h, nw, wuT, w, cos, sin, cos0, sin0, jnp.asarray(xi, F32), jnp.asarray(ks, F32))


def _s5_body(u_ref, prow_ref, bx_ref, cx_ref, d_ref, y_ref, taps_ref, m_ref, *, scan_steps):
    slabs_per_group = SSM_GROUP // SUBLANES
    for gg in range(prow_ref.shape[0]):
        _s5_group(u_ref, prow_ref.at[gg], bx_ref.at[gg], cx_ref.at[gg], d_ref.at[gg], y_ref,
                  taps_ref.at[gg], m_ref.at[gg], slab0=gg * slabs_per_group, scan_steps=scan_steps)


def _s5_group(u_ref, prow_ref, bx_ref, cx_ref, d_ref, y_ref, taps_ref, m_ref, *, slab0, scan_steps):
    batch = u_ref.shape[0]
    n = u_ref.shape[2] // SUBLANES
    t = S5_CHUNK
    half = LANES // 2
    lane = lax.broadcasted_iota(jnp.int32, (1, LANES), 1)
    lo = lane < half
    sgn = jnp.where(lo, -1.0, 1.0)

    def c_form(x):
        return jnp.where(lo, x, pltpu.roll(x, half, 1))

    def s_form(x):
        return jnp.where(lo, -pltpu.roll(x, half, 1), x)

    lr, li = prow_ref[0:1, :], prow_ref[1:2, :]
    step = jnp.exp(prow_ref[2:3, :])
    mag = jnp.exp(lr * step)
    ar, ai = mag * jnp.cos(li * step), mag * jnp.sin(li * step)
    den = lr * lr + li * li
    nr = ar - 1.0
    kr = (nr * lr + ai * li) / den
    ki = (ai * lr - nr * li) / den
    squares = [(ar, ai * sgn)]
    for _ in range(S5_CHUNK_BITS + scan_steps):
        c, s = squares[-1]
        squares.append((c * c - s * s, 2.0 * c * s))

    def powers(expo):
        x = jnp.broadcast_to(jnp.where(lo, 1.0, 0.0), (t, LANES))
        for j in range(S5_CHUNK_BITS):
            x = jnp.where(((expo >> j) & 1) == 1, _rot_half(x, *squares[j]), x)
        return x

    row_id = lax.broadcasted_iota(jnp.int32, (t, LANES), 0)
    a_tau = powers(row_id)
    a_rev = powers(t - 1 - row_id)
    a_next = _rot_half(a_tau, *squares[0])

    bbar = _rot_half(bx_ref[...], kr, ki * sgn)
    bb_c, bb_s = c_form(bbar), s_form(bbar)
    cmat = cx_ref[...]
    cc_c, cc_s = c_form(cmat), s_form(cmat)

    g1 = jnp.concatenate(
        [_rot_half(cmat, bb_c[ci:ci + 1], bb_s[ci:ci + 1]) * (-sgn) for ci in range(SSM_GROUP)], axis=0)
    taps_ref[...] = _dot_nt(g1, a_tau, precision=lax.Precision.HIGHEST)
    bs = jnp.concatenate(
        [_rot_half(a_rev, bb_c[ci:ci + 1], bb_s[ci:ci + 1]) for ci in range(SSM_GROUP)],
        axis=0).astype(BF16)
    cxt = jnp.concatenate(
        [_rot_half(a_next, cc_c[co:co + 1], cc_s[co:co + 1]) * (-sgn) for co in range(SSM_GROUP)],
        axis=0).astype(BF16)

    def channel(ref, b, c):
        return ref.at[b, slab0 + c // SUBLANES, pl.ds(c % SUBLANES, n, stride=SUBLANES), :]

    u = jnp.concatenate(
        [jnp.concatenate([channel(u_ref, b, c)[...].astype(BF16) for c in range(SSM_GROUP)], axis=1)
         for b in range(batch)], axis=0)
    x = _dot(u, bs)
    rows = lax.broadcasted_iota(jnp.int32, x.shape, 0) % n
    for k in range(scan_steps):
        sh = 1 << k
        prev = jnp.where(rows >= sh, pltpu.roll(x, sh, 0), 0.0)
        x = x + _rot_half(prev, *squares[S5_CHUNK_BITS + k])
    h_prev = jnp.where(rows >= 1, pltpu.roll(x, 1, 0), 0.0).astype(BF16)

    causal = (lax.broadcasted_iota(jnp.int32, (t, t), 1)
              >= lax.broadcasted_iota(jnp.int32, (t, t), 0))
    pair = 2 * t
    for jp in range(SSM_GROUP // 2):
        for ci in range(SSM_GROUP):
            for co in range(2 * jp, 2 * jp + 2):
                tap_row = taps_ref[ci * SSM_GROUP + co:ci * SSM_GROUP + co + 1, :]
                blk = pltpu.roll(jnp.broadcast_to(tap_row, (t, t)), 0, 1, stride=1, stride_axis=0)
                m_ref[ci * t:(ci + 1) * t, co * t:(co + 1) * t] = (
                    jnp.where(causal, blk, 0.0).astype(BF16))
        y2 = (_dot(u, m_ref[:, jp * pair:(jp + 1) * pair])
              + _dot_nt(h_prev, cxt[jp * pair:(jp + 1) * pair, :]))
        for co in range(2 * jp, 2 * jp + 2):
            for b in range(batch):
                skip = d_ref[co:co + 1, :] * channel(u_ref, b, co)[...]
                y = y2[b * n:(b + 1) * n, (co - 2 * jp) * t:(co - 2 * jp + 1) * t] + skip
                channel(y_ref, b, co)[...] = jax.nn.gelu(y)


def _s5(u4, prow, bx, cx, dB):
    batch, slabs, rows, _ = u4.shape
    groups = prow.shape[0]
    n_chunks = rows // SUBLANES
    scan_steps = max(n_chunks - 1, 0).bit_length()
    width = SSM_GROUP * S5_CHUNK
    gps = S5_GROUPS_PER_STEP
    per = slabs // groups * gps
    grp = lambda *shape: pl.BlockSpec((gps,) + shape, lambda g: (g,) + (0,) * len(shape))
    io = pl.BlockSpec((batch, per, rows, LANES), lambda g: (0, g, 0, 0))
    return pl.pallas_call(
        functools.partial(_s5_body, scan_steps=scan_steps),
        grid=(groups // gps,),
        in_specs=[io, grp(3, LANES), grp(SSM_GROUP, LANES), grp(SSM_GROUP, LANES),
                  grp(SSM_GROUP, LANES)],
        out_specs=io,
        out_shape=jax.ShapeDtypeStruct(u4.shape, F32),
        scratch_shapes=[pltpu.VMEM((gps, SSM_GROUP * SSM_GROUP, S5_CHUNK), F32),
                        pltpu.VMEM((gps, width, width), BF16)],
        compiler_params=_params(1),
        name="s5",
    )(u4, prow, bx, cx, dB)


def _mix_ffn_body(h_ref, y_ref, yb_ref, ga_ref, gb_ref, wgT_ref, bg_ref, wpa_ref, wpb_ref,
                  wo_ref, nm_ref, wup_ref, wdn_ref, nf_ref, o_ref, *, final):
    slabs = y_ref.shape[0]
    chunks = y_ref.shape[1] // SUBLANES
    yT = jnp.concatenate(
        [jnp.concatenate([y_ref[r, j * SUBLANES:(j + 1) * SUBLANES, :] for j in range(chunks)], axis=1)
         for r in range(slabs)], axis=0)
    gate = jax.nn.sigmoid(_dot(wgT_ref[...], yT.astype(BF16)) + bg_ref[...])
    yaT = (yT * gate).astype(BF16)
    merged = (jax.nn.sigmoid(ga_ref[...].astype(F32)) * _dot_tn(yaT, wpa_ref[...])
              + jax.nn.sigmoid(gb_ref[...].astype(F32)) * _dot(yb_ref[...], wpb_ref[...]))
    h = h_ref[...] + _dot(merged.astype(BF16), wo_ref[...])
    z = _rmsnorm(h, nm_ref[...]).astype(BF16)
    ff = wup_ref.shape[1]
    step = 1024
    for j in range(0, ff, step):
        a = jnp.square(jnp.maximum(_dot(z, wup_ref[:, j:j + step]), 0.0)).astype(BF16)
        h = h + _dot(a, wdn_ref[j:j + step, :])
    if final:
        h = _rmsnorm(h, nf_ref[...])
    o_ref[...] = h


def _mix_ffn(h, y4, yb, ga, gb, wgT, bg, wpa, wpb, wo, nm, wup, wdn, nf, batch, seq, final):
    n, d = h.shape
    tm = min(TOKEN_TILE, seq)
    nt = seq // tm
    slabs = y4.shape[1]
    row = lambda w: pl.BlockSpec((tm, w), lambda i: (i, 0))
    consts = (wgT, bg, wpa, wpb, wo, nm, wup, wdn, nf)
    return pl.pallas_call(
        functools.partial(_mix_ffn_body, final=final),
        grid=(n // tm,),
        in_specs=[row(d),
                  pl.BlockSpec((None, slabs, tm // LANES * SUBLANES, LANES),
                               lambda i: (i // nt, 0, i % nt, 0)),
                  row(yb.shape[1]), row(d), row(d)] + [_whole(w.shape) for w in consts],
        out_specs=row(d),
        out_shape=jax.ShapeDtypeStruct((n, d), F32),
        compiler_params=_params(1),
        name="mix_ffn",
    )(h, y4, yb, ga, gb, *consts)


def _rotary_tables(pos):
    inv_freq = ROPE_BASE ** (-jnp.arange(0, RET_DK, 2, dtype=F32) / RET_DK)
    ang = pos[:, None] * inv_freq[None, :]
    cos, sin = jnp.cos(ang), jnp.sin(ang)
    return jnp.concatenate([cos, cos], axis=1), jnp.concatenate([-sin, sin], axis=1)


def _dup(x):
    return jnp.concatenate([x, x], axis=-1)


def kernel(x, w_in, lam_re, lam_im, b_re, b_im, c_re, c_im, d_skip, log_step, w_glu, b_glu,
           w_proj_ssm, w_proj_ret, w_out, norm_mix, norm_mlp, w_up, w_down, norm_final):
    batch, seq, d = x.shape
    depth, groups, p = lam_re.shape
    sw = w_glu.shape[1]
    assert p == SSM_STATE and sw == groups * SSM_GROUP and 2 * p == LANES
    assert seq % S5_CHUNK == 0 and seq % min(TOKEN_TILE, seq) == 0
    assert min(TOKEN_TILE, seq) % min(RET_CHUNK, seq) == 0
    h = x.reshape(batch * seq, d).astype(F32)
    for l in range(depth):
        w = w_in[l].astype(BF16)
        u4, yb, ga, gb = _in_proj(
            h, norm_mix[l].reshape(1, d).astype(F32), w[:, :sw].T, w, batch, seq)
        prow = jnp.stack([_dup(lam_re[l]), _dup(lam_im[l]),
                          jnp.broadcast_to(log_step[l][:, None], (groups, LANES))], axis=1).astype(F32)
        bx = jnp.concatenate([b_re[l], b_im[l]], axis=1).transpose(0, 2, 1).astype(F32)
        cx = jnp.concatenate([c_re[l], c_im[l]], axis=2).astype(F32)
        dB = jnp.broadcast_to(d_skip[l].reshape(groups, SSM_GROUP, 1), (groups, SSM_GROUP, LANES))
        y4 = _s5(u4, prow, bx, cx, dB.astype(F32))
        h = _mix_ffn(
            h, y4, yb, ga, gb,
            w_glu[l].T.astype(BF16), b_glu[l].reshape(sw, 1).astype(F32),
            w_proj_ssm[l].astype(BF16), w_proj_ret[l].astype(BF16), w_out[l].astype(BF16),
            norm_mlp[l].reshape(1, d).astype(F32), w_up[l].astype(BF16), w_down[l].astype(BF16),
            norm_final.reshape(1, d).astype(F32), batch, seq, final=(l == depth - 1))
    return h.reshape(batch, seq, d).astype(x.dtype)
```

```python
import functools

import numpy as np
import jax
import jax.numpy as jnp
from jax import lax
from jax.experimental import pallas as pl
from jax.experimental.pallas import tpu as pltpu

F32 = jnp.float32
BF16 = jnp.bfloat16

EPS = 1e-6
SSM_GROUP = 16
SSM_STATE = 64
RET_HEADS = 4
RET_DK = 128
RET_DV = 256
ROPE_BASE = 10000.0

LANES = 128
SUBLANES = 8
TILE_PITCH = 10
S5_CHUNK = LANES
S5_CHUNK_BITS = 7
S5_GROUPS_PER_STEP = 2
RET_CHUNK = 256
TOKEN_TILE = 512
VMEM_LIMIT = 56 * 1024 * 1024


def _params(n_axes, vmem=VMEM_LIMIT):
    return pltpu.CompilerParams(
        dimension_semantics=("arbitrary",) * n_axes, vmem_limit_bytes=vmem)


def _whole(shape):
    zeros = (0,) * len(shape)
    return pl.BlockSpec(shape, lambda *_: zeros, pipeline_mode=pl.Buffered(1))


def _dot(a, b):
    return jnp.dot(a, b, preferred_element_type=F32)


def _dot_nt(a, b, precision=None):
    return lax.dot_general(a, b, (((1,), (1,)), ((), ())), precision=precision,
                           preferred_element_type=F32)


def _dot_tn(a, b):
    return lax.dot_general(a, b, (((0,), (0,)), ((), ())), preferred_element_type=F32)


def _rmsnorm(x, w):
    return x * lax.rsqrt(jnp.mean(x * x, axis=-1, keepdims=True) + EPS) * w


def _rot_half(x, c, s):
    return x * c + pltpu.roll(x, LANES // 2, 1) * s


def _in_proj_body(h_ref, nw_ref, wuT_ref, w_ref, cos_ref, sin_ref, cos0_ref, sin0_ref, xi_ref, ks_ref,
                  u_ref, o_ref, ga_ref, gb_ref, state_ref, *, chunk_decay, tiles_per_seq):
    @pl.when(pl.program_id(0) % tiles_per_seq == 0)
    def _():
        state_ref[...] = jnp.zeros_like(state_ref)

    z = _rmsnorm(h_ref[...], nw_ref[...]).astype(BF16)
    uT = _dot_nt(wuT_ref[...], z)
    chunks = uT.shape[1] // LANES
    for r in range(uT.shape[0] // SUBLANES):
        for pad in range(SUBLANES, TILE_PITCH):
            u_ref[r, pl.ds(pad, chunks, stride=TILE_PITCH), :] = jnp.zeros((chunks, LANES), F32)
        for j in range(chunks):
            u_ref[r, j * TILE_PITCH:j * TILE_PITCH + SUBLANES, :] = (
                uT[r * SUBLANES:(r + 1) * SUBLANES, j * LANES:(j + 1) * LANES])
    cos = cos_ref[...] * cos0_ref[...] - sin_ref[...] * sin0_ref[...]
    sin = sin_ref[...] * cos0_ref[...] + cos_ref[...] * sin0_ref[...]
    base = uT.shape[0]
    qk = RET_HEADS * RET_DK
    vw = RET_HEADS * RET_DV
    d = ga_ref.shape[1]
    q_all = _dot(z, w_ref[:, base:base + qk])
    k_all = _dot(z, w_ref[:, base + qk:base + 2 * qk])
    base += 2 * qk
    v_all = _dot(z, w_ref[:, base:base + vw]).astype(BF16)
    g_all = _dot(z, w_ref[:, base + vw:base + 2 * vw])
    base += 2 * vw
    gate_dots = [(ga_ref, base), (gb_ref, base + d)]

    c = xi_ref.shape[1]
    tm = z.shape[0]
    causal = (lax.broadcasted_iota(jnp.int32, (c, c), 0)
              >= lax.broadcasted_iota(jnp.int32, (c, c), 1))
    for sub in range(tm // c):
        rows = slice(sub * c, (sub + 1) * c)
        for hh in range(RET_HEADS):
            qk_cols = slice(hh * RET_DK, (hh + 1) * RET_DK)
            v_cols = slice(hh * RET_DV, (hh + 1) * RET_DV)
            q = (_rot_half(q_all[rows, qk_cols], cos[rows], sin[rows]) * xi_ref[hh]).astype(BF16)
            k = (_rot_half(k_all[rows, qk_cols], cos[rows], sin[rows]) * ks_ref[hh]).astype(BF16)
            v = v_all[rows, v_cols]
            state = state_ref[hh]
            scores = jnp.where(causal, _dot_nt(q, k), 0.0).astype(BF16)
            o = _dot(jnp.concatenate([scores, q], axis=1),
                     jnp.concatenate([v, state.astype(BF16)], axis=0))
            state_ref[hh] = chunk_decay[hh] * (state + _dot_tn(k, v))
            o = o * lax.rsqrt(jnp.mean(o * o, axis=-1, keepdims=True) + EPS)
            o_ref[rows, v_cols] = (o * jax.nn.silu(g_all[rows, v_cols])).astype(BF16)
        if gate_dots:
            gate_ref, col = gate_dots.pop(0)
            gate_ref[...] = _dot(z, w_ref[:, col:col + d]).astype(BF16)
    for gate_ref, col in gate_dots:
        gate_ref[...] = _dot(z, w_ref[:, col:col + d]).astype(BF16)


def _in_proj(h, nw, wuT, w, batch, seq):
    n, d = h.shape
    tm = min(TOKEN_TILE, seq)
    nt = seq // tm
    cos, sin = _rotary_tables(jnp.arange(tm, dtype=F32))
    cos0, sin0 = _rotary_tables(jnp.arange(nt, dtype=F32) * tm)
    cos0, sin0 = cos0.reshape(nt, 1, LANES), sin0.reshape(nt, 1, LANES)
    sw = wuT.shape[0]
    vw = RET_HEADS * RET_DV
    c = min(RET_CHUNK, tm)
    gamma = 1.0 - 2.0 ** (-5.0 - np.arange(RET_HEADS, dtype=np.float64))
    idx = np.arange(c, dtype=np.float64)
    xi = np.broadcast_to((gamma[:, None] ** (idx + 1.0))[:, :, None], (RET_HEADS, c, RET_DK))
    ks = np.broadcast_to((RET_DK ** -0.5 * gamma[:, None] ** (-1.0 - idx))[:, :, None],
                         (RET_HEADS, c, RET_DK))
    chunk_decay = tuple(float(x) for x in gamma ** c)
    row = lambda wd: pl.BlockSpec((tm, wd), lambda i: (i, 0))
    tab0 = pl.BlockSpec((None, 1, LANES), lambda i: (i % nt, 0, 0))
    slabs = sw // SUBLANES
    rows = seq // LANES * TILE_PITCH
    out_shape = (
        jax.ShapeDtypeStruct((batch, slabs, rows, LANES), F32),
        jax.ShapeDtypeStruct((n, vw), BF16),
        jax.ShapeDtypeStruct((n, d), BF16), jax.ShapeDtypeStruct((n, d), BF16),
    )
    return pl.pallas_call(
        functools.partial(_in_proj_body, chunk_decay=chunk_decay, tiles_per_seq=nt),
        grid=(n // tm,),
        in_specs=[row(d), _whole(nw.shape), _whole(wuT.shape), _whole(w.shape),
                  _whole(cos.shape), _whole(sin.shape), tab0, tab0,
                  _whole(xi.shape), _whole(ks.shape)],
        out_specs=(pl.BlockSpec((None, slabs, tm // LANES * TILE_PITCH, LANES),
                                lambda i: (i // nt, 0, i % nt, 0)),
                   row(vw), row(d), row(d)),
        out_shape=out_shape,
        scratch_shapes=[pltpu.VMEM((RET_HEADS, RET_DK, RET_DV), F32)],
        compiler_params=_params(1),
        name="in_proj",
    )(h, nw, wuT, w, cos, sin, cos0, sin0, jnp.asarray(xi, F32), jnp.asarray(ks, F32))


def _s5_body(u_ref, prow_ref, bx_ref, cx_ref, d_ref, y_ref, taps_ref, m_ref, *, scan_steps):
    slabs_per_group = SSM_GROUP // SUBLANES
    for gg in range(prow_ref.shape[0]):
        _s5_group(u_ref, prow_ref.at[gg], bx_ref.at[gg], cx_ref.at[gg], d_ref.at[gg], y_ref,
                  taps_ref.at[gg], m_ref.at[gg], slab0=gg * slabs_per_group, scan_steps=scan_steps)


def _s5_group(u_ref, prow_ref, bx_ref, cx_ref, d_ref, y_ref, taps_ref, m_ref, *, slab0, scan_steps):
    batch = u_ref.shape[0]
    n = u_ref.shape[2] // TILE_PITCH
    t = S5_CHUNK
    half = LANES // 2
    lane = lax.broadcasted_iota(jnp.int32, (1, LANES), 1)
    lo = lane < half
    sgn = jnp.where(lo, -1.0, 1.0)

    def c_form(x):
        return jnp.where(lo, x, pltpu.roll(x, half, 1))

    def s_form(x):
        return jnp.where(lo, -pltpu.roll(x, half, 1), x)

    lr, li = prow_ref[0:1, :], prow_ref[1:2, :]
    step = jnp.exp(prow_ref[2:3, :])
    mag = jnp.exp(lr * step)
    ar, ai = mag * jnp.cos(li * step), mag * jnp.sin(li * step)
    den = lr * lr + li * li
    nr = ar - 1.0
    kr = (nr * lr + ai * li) / den
    ki = (ai * lr - nr * li) / den
    squares = [(ar, ai * sgn)]
    for _ in range(S5_CHUNK_BITS + scan_steps):
        c, s = squares[-1]
        squares.append((c * c - s * s, 2.0 * c * s))

    def powers(expo):
        x = jnp.broadcast_to(jnp.where(lo, 1.0, 0.0), (t, LANES))
        for j in range(S5_CHUNK_BITS):
            x = jnp.where(((expo >> j) & 1) == 1, _rot_half(x, *squares[j]), x)
        return x

    row_id = lax.broadcasted_iota(jnp.int32, (t, LANES), 0)
    a_tau = powers(row_id)
    a_rev = powers(t - 1 - row_id)
    a_next = _rot_half(a_tau, *squares[0])

    bbar = _rot_half(bx_ref[...], kr, ki * sgn)
    bb_c, bb_s = c_form(bbar), s_form(bbar)
    cmat = cx_ref[...]
    cc_c, cc_s = c_form(cmat), s_form(cmat)

    g1 = jnp.concatenate(
        [_rot_half(cmat, bb_c[ci:ci + 1], bb_s[ci:ci + 1]) * (-sgn) for ci in range(SSM_GROUP)], axis=0)
    taps_ref[...] = _dot_nt(g1, a_tau, precision=lax.Precision.HIGHEST)
    bs = jnp.concatenate(
        [_rot_half(a_rev, bb_c[ci:ci + 1], bb_s[ci:ci + 1]) for ci in range(SSM_GROUP)],
        axis=0).astype(BF16)
    cxt = jnp.concatenate(
        [_rot_half(a_next, cc_c[co:co + 1], cc_s[co:co + 1]) * (-sgn) for co in range(SSM_GROUP)],
        axis=0).astype(BF16)

    def channel(ref, b, c):
        return ref.at[b, slab0 + c // SUBLANES, pl.ds(c % SUBLANES, n, stride=TILE_PITCH), :]

    u = jnp.concatenate(
        [jnp.concatenate([channel(u_ref, b, c)[...].astype(BF16) for c in range(SSM_GROUP)], axis=1)
         for b in range(batch)], axis=0)
    x = _dot(u, bs)
    rows = lax.broadcasted_iota(jnp.int32, x.shape, 0) % n
    for k in range(scan_steps):
        sh = 1 << k
        prev = jnp.where(rows >= sh, pltpu.roll(x, sh, 0), 0.0)
        x = x + _rot_half(prev, *squares[S5_CHUNK_BITS + k])
    h_prev = jnp.where(rows >= 1, pltpu.roll(x, 1, 0), 0.0).astype(BF16)

    causal = (lax.broadcasted_iota(jnp.int32, (t, t), 1)
              >= lax.broadcasted_iota(jnp.int32, (t, t), 0))
    pair = 2 * t
    for b in range(batch):
        for slab in range(slab0, slab0 + SSM_GROUP // SUBLANES):
            for pad in range(SUBLANES, TILE_PITCH):
                y_ref[b, slab, pl.ds(pad, n, stride=TILE_PITCH), :] = jnp.zeros((n, LANES), F32)
    for jp in range(SSM_GROUP // 2):
        for ci in range(SSM_GROUP):
            for co in range(2 * jp, 2 * jp + 2):
                tap_row = taps_ref[ci * SSM_GROUP + co:ci * SSM_GROUP + co + 1, :]
                blk = pltpu.roll(jnp.broadcast_to(tap_row, (t, t)), 0, 1, stride=1, stride_axis=0)
                m_ref[ci * t:(ci + 1) * t, co * t:(co + 1) * t] = (
                    jnp.where(causal, blk, 0.0).astype(BF16))
        y2 = (_dot(u, m_ref[:, jp * pair:(jp + 1) * pair])
              + _dot_nt(h_prev, cxt[jp * pair:(jp + 1) * pair, :]))
        for co in range(2 * jp, 2 * jp + 2):
            for b in range(batch):
                skip = d_ref[co:co + 1, :] * channel(u_ref, b, co)[...]
                y = y2[b * n:(b + 1) * n, (co - 2 * jp) * t:(co - 2 * jp + 1) * t] + skip
                channel(y_ref, b, co)[...] = jax.nn.gelu(y)


def _s5(u4, prow, bx, cx, dB):
    batch, slabs, rows, _ = u4.shape
    groups = prow.shape[0]
    n_chunks = rows // TILE_PITCH
    scan_steps = max(n_chunks - 1, 0).bit_length()
    width = SSM_GROUP * S5_CHUNK
    gps = S5_GROUPS_PER_STEP
    per = slabs // groups * gps
    grp = lambda *shape: pl.BlockSpec((gps,) + shape, lambda g: (g,) + (0,) * len(shape))
    io = pl.BlockSpec((batch, per, rows, LANES), lambda g: (0, g, 0, 0))
    return pl.pallas_call(
        functools.partial(_s5_body, scan_steps=scan_steps),
        grid=(groups // gps,),
        in_specs=[io, grp(3, LANES), grp(SSM_GROUP, LANES), grp(SSM_GROUP, LANES),
                  grp(SSM_GROUP, LANES)],
        out_specs=io,
        out_shape=jax.ShapeDtypeStruct(u4.shape, F32),
        scratch_shapes=[pltpu.VMEM((gps, SSM_GROUP * SSM_GROUP, S5_CHUNK), F32),
                        pltpu.VMEM((gps, width, width), BF16)],
        compiler_params=_params(1),
        name="s5",
    )(u4, prow, bx, cx, dB)


def _mix_ffn_body(h_ref, y_ref, yb_ref, ga_ref, gb_ref, wgT_ref, bg_ref, wpa_ref, wpb_ref,
                  wo_ref, nm_ref, wup_ref, wdn_ref, nf_ref, o_ref, *, final):
    slabs = y_ref.shape[0]
    chunks = y_ref.shape[1] // TILE_PITCH
    yT = jnp.concatenate(
        [jnp.concatenate([y_ref[r, j * TILE_PITCH:j * TILE_PITCH + SUBLANES, :] for j in range(chunks)],
                         axis=1)
         for r in range(slabs)], axis=0)
    gate = jax.nn.sigmoid(_dot(wgT_ref[...], yT.astype(BF16)) + bg_ref[...])
    yaT = (yT * gate).astype(BF16)
    merged = (jax.nn.sigmoid(ga_ref[...].astype(F32)) * _dot_tn(yaT, wpa_ref[...])
              + jax.nn.sigmoid(gb_ref[...].astype(F32)) * _dot(yb_ref[...], wpb_ref[...]))
    h = h_ref[...] + _dot(merged.astype(BF16), wo_ref[...])
    z = _rmsnorm(h, nm_ref[...]).astype(BF16)
    ff = wup_ref.shape[1]
    step = 1024
    for j in range(0, ff, step):
        a = jnp.square(jnp.maximum(_dot(z, wup_ref[:, j:j + step]), 0.0)).astype(BF16)
        h = h + _dot(a, wdn_ref[j:j + step, :])
    if final:
        h = _rmsnorm(h, nf_ref[...])
    o_ref[...] = h


def _mix_ffn(h, y4, yb, ga, gb, wgT, bg, wpa, wpb, wo, nm, wup, wdn, nf, batch, seq, final):
    n, d = h.shape
    tm = min(TOKEN_TILE, seq)
    nt = seq // tm
    slabs = y4.shape[1]
    row = lambda w: pl.BlockSpec((tm, w), lambda i: (i, 0))
    consts = (wgT, bg, wpa, wpb, wo, nm, wup, wdn, nf)
    return pl.pallas_call(
        functools.partial(_mix_ffn_body, final=final),
        grid=(n // tm,),
        in_specs=[row(d),
                  pl.BlockSpec((None, slabs, tm // LANES * TILE_PITCH, LANES),
                               lambda i: (i // nt, 0, i % nt, 0)),
                  row(yb.shape[1]), row(d), row(d)] + [_whole(w.shape) for w in consts],
        out_specs=row(d),
        out_shape=jax.ShapeDtypeStruct((n, d), F32),
        compiler_params=_params(1),
        name="mix_ffn",
    )(h, y4, yb, ga, gb, *consts)


def _rotary_tables(pos):
    inv_freq = ROPE_BASE ** (-jnp.arange(0, RET_DK, 2, dtype=F32) / RET_DK)
    ang = pos[:, None] * inv_freq[None, :]
    cos, sin = jnp.cos(ang), jnp.sin(ang)
    return jnp.concatenate([cos, cos], axis=1), jnp.concatenate([-sin, sin], axis=1)


def _dup(x):
    return jnp.concatenate([x, x], axis=-1)


def kernel(x, w_in, lam_re, lam_im, b_re, b_im, c_re, c_im, d_skip, log_step, w_glu, b_glu,
           w_proj_ssm, w_proj_ret, w_out, norm_mix, norm_mlp, w_up, w_down, norm_final):
    batch, seq, d = x.shape
    depth, groups, p = lam_re.shape
    sw = w_glu.shape[1]
    assert p == SSM_STATE and sw == groups * SSM_GROUP and 2 * p == LANES
    assert seq % S5_CHUNK == 0 and seq % min(TOKEN_TILE, seq) == 0
    assert min(TOKEN_TILE, seq) % min(RET_CHUNK, seq) == 0
    h = x.reshape(batch * seq, d).astype(F32)
    for l in range(depth):
        w = w_in[l].astype(BF16)
        u4, yb, ga, gb = _in_proj(
            h, norm_mix[l].reshape(1, d).astype(F32), w[:, :sw].T, w, batch, seq)
        prow = jnp.stack([_dup(lam_re[l]), _dup(lam_im[l]),
                          jnp.broadcast_to(log_step[l][:, None], (groups, LANES))], axis=1).astype(F32)
        bx = jnp.concatenate([b_re[l], b_im[l]], axis=1).transpose(0, 2, 1).astype(F32)
        cx = jnp.concatenate([c_re[l], c_im[l]], axis=2).astype(F32)
        dB = jnp.broadcast_to(d_skip[l].reshape(groups, SSM_GROUP, 1), (groups, SSM_GROUP, LANES))
        y4 = _s5(u4, prow, bx, cx, dB.astype(F32))
        h = _mix_ffn(
            h, y4, yb, ga, gb,
            w_glu[l].T.astype(BF16), b_glu[l].reshape(sw, 1).astype(F32),
            w_proj_ssm[l].astype(BF16), w_proj_ret[l].astype(BF16), w_out[l].astype(BF16),
            norm_mlp[l].reshape(1, d).astype(F32), w_up[l].astype(BF16), w_down[l].astype(BF16),
            norm_final.reshape(1, d).astype(F32), batch, seq, final=(l == depth - 1))
    return h.reshape(batch, seq, d).astype(x.dtype)
```

```python
import functools

import numpy as np
import jax
import jax.numpy as jnp
from jax import lax
from jax.experimental import pallas as pl
from jax.experimental.pallas import tpu as pltpu

F32 = jnp.float32
BF16 = jnp.bfloat16

EPS = 1e-6
SSM_GROUP = 16
SSM_STATE = 64
RET_HEADS = 4
RET_DK = 128
RET_DV = 256
ROPE_BASE = 10000.0

LANES = 128
SUBLANES = 8
TILE_PITCH = 10
S5_CHUNK = LANES
S5_CHUNK_BITS = 7
S5_GROUPS_PER_STEP = 2
RET_CHUNK = 256
TOKEN_TILE = 512
VMEM_LIMIT = 56 * 1024 * 1024


def _params(n_axes, vmem=VMEM_LIMIT):
    return pltpu.CompilerParams(
        dimension_semantics=("arbitrary",) * n_axes, vmem_limit_bytes=vmem)


def _whole(shape):
    zeros = (0,) * len(shape)
    return pl.BlockSpec(shape, lambda *_: zeros, pipeline_mode=pl.Buffered(1))


def _layer(stack, l):
    zeros = (0,) * (stack.ndim - 1)
    return pl.BlockSpec((None,) + stack.shape[1:], lambda *_: (l,) + zeros,
                        pipeline_mode=pl.Buffered(1))


def _dot(a, b):
    return jnp.dot(a, b, preferred_element_type=F32)


def _dot_nt(a, b, precision=None):
    return lax.dot_general(a, b, (((1,), (1,)), ((), ())), precision=precision,
                           preferred_element_type=F32)


def _dot_tn(a, b):
    return lax.dot_general(a, b, (((0,), (0,)), ((), ())), preferred_element_type=F32)


def _rmsnorm(x, w):
    return x * lax.rsqrt(jnp.mean(x * x, axis=-1, keepdims=True) + EPS) * w


def _rot_half(x, c, s):
    return x * c + pltpu.roll(x, LANES // 2, 1) * s


def _in_proj_body(h_ref, nw_ref, wuT_ref, w_ref, cos_ref, sin_ref, cos0_ref, sin0_ref, xi_ref, ks_ref,
                  u_ref, o_ref, ga_ref, gb_ref, state_ref, *, chunk_decay, tiles_per_seq):
    @pl.when(pl.program_id(0) % tiles_per_seq == 0)
    def _():
        state_ref[...] = jnp.zeros_like(state_ref)

    z = _rmsnorm(h_ref[...], nw_ref[...]).astype(BF16)
    uT = _dot_nt(wuT_ref[...], z)
    chunks = uT.shape[1] // LANES
    for r in range(uT.shape[0] // SUBLANES):
        for pad in range(SUBLANES, TILE_PITCH):
            u_ref[r, pl.ds(pad, chunks, stride=TILE_PITCH), :] = jnp.zeros((chunks, LANES), F32)
        for j in range(chunks):
            u_ref[r, j * TILE_PITCH:j * TILE_PITCH + SUBLANES, :] = (
                uT[r * SUBLANES:(r + 1) * SUBLANES, j * LANES:(j + 1) * LANES])
    cos = cos_ref[...] * cos0_ref[...] - sin_ref[...] * sin0_ref[...]
    sin = sin_ref[...] * cos0_ref[...] + cos_ref[...] * sin0_ref[...]
    base = uT.shape[0]
    qk = RET_HEADS * RET_DK
    vw = RET_HEADS * RET_DV
    d = ga_ref.shape[1]
    q_all = _dot(z, w_ref[:, base:base + qk])
    k_all = _dot(z, w_ref[:, base + qk:base + 2 * qk])
    base += 2 * qk
    v_all = _dot(z, w_ref[:, base:base + vw]).astype(BF16)
    g_all = _dot(z, w_ref[:, base + vw:base + 2 * vw])
    base += 2 * vw
    gate_dots = [(ga_ref, base), (gb_ref, base + d)]

    c = xi_ref.shape[1]
    tm = z.shape[0]
    causal = (lax.broadcasted_iota(jnp.int32, (c, c), 0)
              >= lax.broadcasted_iota(jnp.int32, (c, c), 1))
    for sub in range(tm // c):
        rows = slice(sub * c, (sub + 1) * c)
        for hh in range(RET_HEADS):
            qk_cols = slice(hh * RET_DK, (hh + 1) * RET_DK)
            v_cols = slice(hh * RET_DV, (hh + 1) * RET_DV)
            q = (_rot_half(q_all[rows, qk_cols], cos[rows], sin[rows]) * xi_ref[hh]).astype(BF16)
            k = (_rot_half(k_all[rows, qk_cols], cos[rows], sin[rows]) * ks_ref[hh]).astype(BF16)
            v = v_all[rows, v_cols]
            state = state_ref[hh]
            scores = jnp.where(causal, _dot_nt(q, k), 0.0).astype(BF16)
            o = _dot(jnp.concatenate([scores, q], axis=1),
                     jnp.concatenate([v, state.astype(BF16)], axis=0))
            state_ref[hh] = chunk_decay[hh] * (state + _dot_tn(k, v))
            o = o * lax.rsqrt(jnp.mean(o * o, axis=-1, keepdims=True) + EPS)
            o_ref[rows, v_cols] = (o * jax.nn.silu(g_all[rows, v_cols])).astype(BF16)
        if gate_dots:
            gate_ref, col = gate_dots.pop(0)
            gate_ref[...] = _dot(z, w_ref[:, col:col + d]).astype(BF16)
    for gate_ref, col in gate_dots:
        gate_ref[...] = _dot(z, w_ref[:, col:col + d]).astype(BF16)


def _in_proj(h, nw, wuT, w, layer, batch, seq):
    n, d = h.shape
    tm = min(TOKEN_TILE, seq)
    nt = seq // tm
    cos, sin = _rotary_tables(jnp.arange(tm, dtype=F32))
    cos0, sin0 = _rotary_tables(jnp.arange(nt, dtype=F32) * tm)
    cos0, sin0 = cos0.reshape(nt, 1, LANES), sin0.reshape(nt, 1, LANES)
    sw = wuT.shape[1]
    vw = RET_HEADS * RET_DV
    c = min(RET_CHUNK, tm)
    gamma = 1.0 - 2.0 ** (-5.0 - np.arange(RET_HEADS, dtype=np.float64))
    idx = np.arange(c, dtype=np.float64)
    xi = np.broadcast_to((gamma[:, None] ** (idx + 1.0))[:, :, None], (RET_HEADS, c, RET_DK))
    ks = np.broadcast_to((RET_DK ** -0.5 * gamma[:, None] ** (-1.0 - idx))[:, :, None],
                         (RET_HEADS, c, RET_DK))
    chunk_decay = tuple(float(x) for x in gamma ** c)
    row = lambda wd: pl.BlockSpec((tm, wd), lambda i: (i, 0))
    tab0 = pl.BlockSpec((None, 1, LANES), lambda i: (i % nt, 0, 0))
    slabs = sw // SUBLANES
    rows = seq // LANES * TILE_PITCH
    out_shape = (
        jax.ShapeDtypeStruct((batch, slabs, rows, LANES), F32),
        jax.ShapeDtypeStruct((n, vw), BF16),
        jax.ShapeDtypeStruct((n, d), BF16), jax.ShapeDtypeStruct((n, d), BF16),
    )
    return pl.pallas_call(
        functools.partial(_in_proj_body, chunk_decay=chunk_decay, tiles_per_seq=nt),
        grid=(n // tm,),
        in_specs=[row(d), _layer(nw, layer), _layer(wuT, layer), _layer(w, layer),
                  _whole(cos.shape), _whole(sin.shape), tab0, tab0,
                  _whole(xi.shape), _whole(ks.shape)],
        out_specs=(pl.BlockSpec((None, slabs, tm // LANES * TILE_PITCH, LANES),
                                lambda i: (i // nt, 0, i % nt, 0)),
                   row(vw), row(d), row(d)),
        out_shape=out_shape,
        scratch_shapes=[pltpu.VMEM((RET_HEADS, RET_DK, RET_DV), F32)],
        compiler_params=_params(1),
        name="in_proj",
    )(h, nw, wuT, w, cos, sin, cos0, sin0, jnp.asarray(xi, F32), jnp.asarray(ks, F32))


def _s5_body(u_ref, prow_ref, bx_ref, cx_ref, d_ref, y_ref, taps_ref, m_ref, *, scan_steps):
    slabs_per_group = SSM_GROUP // SUBLANES
    for gg in range(prow_ref.shape[0]):
        _s5_group(u_ref, prow_ref.at[gg], bx_ref.at[gg], cx_ref.at[gg], d_ref.at[gg], y_ref,
                  taps_ref.at[gg], m_ref.at[gg], slab0=gg * slabs_per_group, scan_steps=scan_steps)


def _s5_group(u_ref, prow_ref, bx_ref, cx_ref, d_ref, y_ref, taps_ref, m_ref, *, slab0, scan_steps):
    batch = u_ref.shape[0]
    n = u_ref.shape[2] // TILE_PITCH
    t = S5_CHUNK
    half = LANES // 2
    lane = lax.broadcasted_iota(jnp.int32, (1, LANES), 1)
    lo = lane < half
    sgn = jnp.where(lo, -1.0, 1.0)

    def c_form(x):
        return jnp.where(lo, x, pltpu.roll(x, half, 1))

    def s_form(x):
        return jnp.where(lo, -pltpu.roll(x, half, 1), x)

    lr, li = prow_ref[0:1, :], prow_ref[1:2, :]
    step = jnp.exp(prow_ref[2:3, :])
    mag = jnp.exp(lr * step)
    ar, ai = mag * jnp.cos(li * step), mag * jnp.sin(li * step)
    den = lr * lr + li * li
    nr = ar - 1.0
    kr = (nr * lr + ai * li) / den
    ki = (ai * lr - nr * li) / den
    squares = [(ar, ai * sgn)]
    for _ in range(S5_CHUNK_BITS + scan_steps):
        c, s = squares[-1]
        squares.append((c * c - s * s, 2.0 * c * s))

    def powers(expo):
        x = jnp.broadcast_to(jnp.where(lo, 1.0, 0.0), (t, LANES))
        for j in range(S5_CHUNK_BITS):
            x = jnp.where(((expo >> j) & 1) == 1, _rot_half(x, *squares[j]), x)
        return x

    row_id = lax.broadcasted_iota(jnp.int32, (t, LANES), 0)
    a_tau = powers(row_id)
    a_rev = powers(t - 1 - row_id)
    a_next = _rot_half(a_tau, *squares[0])

    bbar = _rot_half(bx_ref[...], kr, ki * sgn)
    bb_c, bb_s = c_form(bbar), s_form(bbar)
    cmat = cx_ref[...]
    cc_c, cc_s = c_form(cmat), s_form(cmat)

    g1 = jnp.concatenate(
        [_rot_half(cmat, bb_c[ci:ci + 1], bb_s[ci:ci + 1]) * (-sgn) for ci in range(SSM_GROUP)], axis=0)
    taps_ref[...] = _dot_nt(g1, a_tau, precision=lax.Precision.HIGHEST)
    bs = jnp.concatenate(
        [_rot_half(a_rev, bb_c[ci:ci + 1], bb_s[ci:ci + 1]) for ci in range(SSM_GROUP)],
        axis=0).astype(BF16)
    cxt = jnp.concatenate(
        [_rot_half(a_next, cc_c[co:co + 1], cc_s[co:co + 1]) * (-sgn) for co in range(SSM_GROUP)],
        axis=0).astype(BF16)

    def channel(ref, b, c):
        return ref.at[b, slab0 + c // SUBLANES, pl.ds(c % SUBLANES, n, stride=TILE_PITCH), :]

    u = jnp.concatenate(
        [jnp.concatenate([channel(u_ref, b, c)[...].astype(BF16) for c in range(SSM_GROUP)], axis=1)
         for b in range(batch)], axis=0)
    x = _dot(u, bs)
    rows = lax.broadcasted_iota(jnp.int32, x.shape, 0) % n
    for k in range(scan_steps):
        sh = 1 << k
        prev = jnp.where(rows >= sh, pltpu.roll(x, sh, 0), 0.0)
        x = x + _rot_half(prev, *squares[S5_CHUNK_BITS + k])
    h_prev = jnp.where(rows >= 1, pltpu.roll(x, 1, 0), 0.0).astype(BF16)

    causal = (lax.broadcasted_iota(jnp.int32, (t, t), 1)
              >= lax.broadcasted_iota(jnp.int32, (t, t), 0))
    pair = 2 * t
    for b in range(batch):
        for slab in range(slab0, slab0 + SSM_GROUP // SUBLANES):
            for pad in range(SUBLANES, TILE_PITCH):
                y_ref[b, slab, pl.ds(pad, n, stride=TILE_PITCH), :] = jnp.zeros((n, LANES), F32)
    for jp in range(SSM_GROUP // 2):
        for ci in range(SSM_GROUP):
            for co in range(2 * jp, 2 * jp + 2):
                tap_row = taps_ref[ci * SSM_GROUP + co:ci * SSM_GROUP + co + 1, :]
                blk = pltpu.roll(jnp.broadcast_to(tap_row, (t, t)), 0, 1, stride=1, stride_axis=0)
                m_ref[ci * t:(ci + 1) * t, co * t:(co + 1) * t] = (
                    jnp.where(causal, blk, 0.0).astype(BF16))
        y2 = (_dot(u, m_ref[:, jp * pair:(jp + 1) * pair])
              + _dot_nt(h_prev, cxt[jp * pair:(jp + 1) * pair, :]))
        for co in range(2 * jp, 2 * jp + 2):
            for b in range(batch):
                skip = d_ref[co:co + 1, :] * channel(u_ref, b, co)[...]
                y = y2[b * n:(b + 1) * n, (co - 2 * jp) * t:(co - 2 * jp + 1) * t] + skip
                channel(y_ref, b, co)[...] = jax.nn.gelu(y)


def _s5(u4, prow, bx, cx, dB):
    batch, slabs, rows, _ = u4.shape
    groups = prow.shape[0]
    n_chunks = rows // TILE_PITCH
    scan_steps = max(n_chunks - 1, 0).bit_length()
    width = SSM_GROUP * S5_CHUNK
    gps = S5_GROUPS_PER_STEP
    per = slabs // groups * gps
    grp = lambda *shape: pl.BlockSpec((gps,) + shape, lambda g: (g,) + (0,) * len(shape))
    io = pl.BlockSpec((batch, per, rows, LANES), lambda g: (0, g, 0, 0))
    return pl.pallas_call(
        functools.partial(_s5_body, scan_steps=scan_steps),
        grid=(groups // gps,),
        in_specs=[io, grp(3, LANES), grp(SSM_GROUP, LANES), grp(SSM_GROUP, LANES),
                  grp(SSM_GROUP, LANES)],
        out_specs=io,
        out_shape=jax.ShapeDtypeStruct(u4.shape, F32),
        scratch_shapes=[pltpu.VMEM((gps, SSM_GROUP * SSM_GROUP, S5_CHUNK), F32),
                        pltpu.VMEM((gps, width, width), BF16)],
        compiler_params=_params(1),
        name="s5",
    )(u4, prow, bx, cx, dB)


def _mix_ffn_body(h_ref, y_ref, yb_ref, ga_ref, gb_ref, wgT_ref, bg_ref, wpa_ref, wpb_ref,
                  wo_ref, nm_ref, wup_ref, wdn_ref, nf_ref, o_ref, *, final):
    slabs = y_ref.shape[0]
    chunks = y_ref.shape[1] // TILE_PITCH
    yT = jnp.concatenate(
        [jnp.concatenate([y_ref[r, j * TILE_PITCH:j * TILE_PITCH + SUBLANES, :] for j in range(chunks)],
                         axis=1)
         for r in range(slabs)], axis=0)
    gate = jax.nn.sigmoid(_dot(wgT_ref[...], yT.astype(BF16)) + bg_ref[...])
    yaT = (yT * gate).astype(BF16)
    merged = (jax.nn.sigmoid(ga_ref[...].astype(F32)) * _dot_tn(yaT, wpa_ref[...])
              + jax.nn.sigmoid(gb_ref[...].astype(F32)) * _dot(yb_ref[...], wpb_ref[...]))
    h = h_ref[...] + _dot(merged.astype(BF16), wo_ref[...])
    z = _rmsnorm(h, nm_ref[...]).astype(BF16)
    ff = wup_ref.shape[1]
    step = 1024
    for j in range(0, ff, step):
        a = jnp.square(jnp.maximum(_dot(z, wup_ref[:, j:j + step]), 0.0)).astype(BF16)
        h = h + _dot(a, wdn_ref[j:j + step, :])
    if final:
        h = _rmsnorm(h, nf_ref[...])
    o_ref[...] = h


def _mix_ffn(h, y4, yb, ga, gb, wgT, bg, wpa, wpb, wo, nm, wup, wdn, nf, layer, batch, seq, final):
    n, d = h.shape
    tm = min(TOKEN_TILE, seq)
    nt = seq // tm
    slabs = y4.shape[1]
    row = lambda w: pl.BlockSpec((tm, w), lambda i: (i, 0))
    stacks = (wgT, bg, wpa, wpb, wo, nm, wup, wdn)
    return pl.pallas_call(
        functools.partial(_mix_ffn_body, final=final),
        grid=(n // tm,),
        in_specs=[row(d),
                  pl.BlockSpec((None, slabs, tm // LANES * TILE_PITCH, LANES),
                               lambda i: (i // nt, 0, i % nt, 0)),
                  row(yb.shape[1]), row(d), row(d)]
        + [_layer(w, layer) for w in stacks] + [_whole(nf.shape)],
        out_specs=row(d),
        out_shape=jax.ShapeDtypeStruct((n, d), F32),
        compiler_params=_params(1),
        name="mix_ffn",
    )(h, y4, yb, ga, gb, *stacks, nf)


def _rotary_tables(pos):
    inv_freq = ROPE_BASE ** (-jnp.arange(0, RET_DK, 2, dtype=F32) / RET_DK)
    ang = pos[:, None] * inv_freq[None, :]
    cos, sin = jnp.cos(ang), jnp.sin(ang)
    return jnp.concatenate([cos, cos], axis=1), jnp.concatenate([-sin, sin], axis=1)


def _dup(x):
    return jnp.concatenate([x, x], axis=-1)


def kernel(x, w_in, lam_re, lam_im, b_re, b_im, c_re, c_im, d_skip, log_step, w_glu, b_glu,
           w_proj_ssm, w_proj_ret, w_out, norm_mix, norm_mlp, w_up, w_down, norm_final):
    batch, seq, d = x.shape
    depth, groups, p = lam_re.shape
    sw = w_glu.shape[1]
    assert p == SSM_STATE and sw == groups * SSM_GROUP and 2 * p == LANES
    assert seq % S5_CHUNK == 0 and seq % min(TOKEN_TILE, seq) == 0
    assert min(TOKEN_TILE, seq) % min(RET_CHUNK, seq) == 0
    h = x.reshape(batch * seq, d).astype(F32)
    w_all = w_in.astype(BF16)
    wuT_all = jnp.swapaxes(w_all[:, :, :sw], 1, 2)
    wgT_all = jnp.swapaxes(w_glu, 1, 2).astype(BF16)
    wpa_all, wpb_all, wo_all = (w.astype(BF16) for w in (w_proj_ssm, w_proj_ret, w_out))
    wup_all, wdn_all = w_up.astype(BF16), w_down.astype(BF16)
    nmix_all = norm_mix.reshape(depth, 1, d).astype(F32)
    nmlp_all = norm_mlp.reshape(depth, 1, d).astype(F32)
    bg_all = b_glu.reshape(depth, sw, 1).astype(F32)
    for l in range(depth):
        u4, yb, ga, gb = _in_proj(h, nmix_all, wuT_all, w_all, l, batch, seq)
        prow = jnp.stack([_dup(lam_re[l]), _dup(lam_im[l]),
                          jnp.broadcast_to(log_step[l][:, None], (groups, LANES))], axis=1).astype(F32)
        bx = jnp.concatenate([b_re[l], b_im[l]], axis=1).transpose(0, 2, 1).astype(F32)
        cx = jnp.concatenate([c_re[l], c_im[l]], axis=2).astype(F32)
        dB = jnp.broadcast_to(d_skip[l].reshape(groups, SSM_GROUP, 1), (groups, SSM_GROUP, LANES))
        y4 = _s5(u4, prow, bx, cx, dB.astype(F32))
        h = _mix_ffn(
            h, y4, yb, ga, gb, wgT_all, bg_all, wpa_all, wpb_all, wo_all, nmlp_all, wup_all, wdn_all,
            norm_final.reshape(1, d).astype(F32), l, batch, seq, final=(l == depth - 1))
    return h.reshape(batch, seq, d).astype(x.dtype)
```

```python
import functools

import numpy as np
import jax
import jax.numpy as jnp
from jax import lax
from jax.experimental import pallas as pl
from jax.experimental.pallas import tpu as pltpu

F32 = jnp.float32
BF16 = jnp.bfloat16

EPS = 1e-6
SSM_GROUP = 16
SSM_STATE = 64
RET_HEADS = 4
RET_DK = 128
RET_DV = 256
ROPE_BASE = 10000.0

LANES = 128
SUBLANES = 8
TILE_PITCH = 10
S5_CHUNK = LANES
S5_CHUNK_BITS = 7
S5_GROUPS_PER_STEP = 2
RET_CHUNK = 256
IN_PROJ_TILE = 1024
MIX_FFN_TILE = 512
VMEM_LIMIT = 56 * 1024 * 1024


def _params(n_axes, vmem=VMEM_LIMIT):
    return pltpu.CompilerParams(
        dimension_semantics=("arbitrary",) * n_axes, vmem_limit_bytes=vmem)


def _whole(shape):
    zeros = (0,) * len(shape)
    return pl.BlockSpec(shape, lambda *_: zeros, pipeline_mode=pl.Buffered(1))


def _layer(stack, l):
    zeros = (0,) * (stack.ndim - 1)
    return pl.BlockSpec((None,) + stack.shape[1:], lambda *_: (l,) + zeros,
                        pipeline_mode=pl.Buffered(1))


def _dot(a, b):
    return jnp.dot(a, b, preferred_element_type=F32)


def _dot_nt(a, b, precision=None):
    return lax.dot_general(a, b, (((1,), (1,)), ((), ())), precision=precision,
                           preferred_element_type=F32)


def _dot_tn(a, b):
    return lax.dot_general(a, b, (((0,), (0,)), ((), ())), preferred_element_type=F32)


def _rmsnorm(x, w):
    return x * lax.rsqrt(jnp.mean(x * x, axis=-1, keepdims=True) + EPS) * w


def _rot_half(x, c, s):
    return x * c + pltpu.roll(x, LANES // 2, 1) * s


def _in_proj_body(h_ref, nw_ref, wuT_ref, w_ref, cos_ref, sin_ref, cos0_ref, sin0_ref, xi_ref, ks_ref,
                  u_ref, o_ref, ga_ref, gb_ref, state_ref, *, chunk_decay, tiles_per_seq):
    @pl.when(pl.program_id(0) % tiles_per_seq == 0)
    def _():
        state_ref[...] = jnp.zeros_like(state_ref)

    z = _rmsnorm(h_ref[...], nw_ref[...]).astype(BF16)
    uT = _dot_nt(wuT_ref[...], z)
    chunks = uT.shape[1] // LANES
    for r in range(uT.shape[0] // SUBLANES):
        for pad in range(SUBLANES, TILE_PITCH):
            u_ref[r, pl.ds(pad, chunks, stride=TILE_PITCH), :] = jnp.zeros((chunks, LANES), F32)
        for j in range(chunks):
            u_ref[r, j * TILE_PITCH:j * TILE_PITCH + SUBLANES, :] = (
                uT[r * SUBLANES:(r + 1) * SUBLANES, j * LANES:(j + 1) * LANES])
    cos = cos_ref[...] * cos0_ref[...] - sin_ref[...] * sin0_ref[...]
    sin = sin_ref[...] * cos0_ref[...] + cos_ref[...] * sin0_ref[...]
    base = uT.shape[0]
    qk = RET_HEADS * RET_DK
    vw = RET_HEADS * RET_DV
    d = ga_ref.shape[1]
    q_all = _dot(z, w_ref[:, base:base + qk])
    k_all = _dot(z, w_ref[:, base + qk:base + 2 * qk])
    base += 2 * qk
    v_all = _dot(z, w_ref[:, base:base + vw]).astype(BF16)
    g_all = _dot(z, w_ref[:, base + vw:base + 2 * vw])
    base += 2 * vw
    gate_dots = [(ga_ref, base), (gb_ref, base + d)]

    c = xi_ref.shape[1]
    tm = z.shape[0]
    causal = (lax.broadcasted_iota(jnp.int32, (c, c), 0)
              >= lax.broadcasted_iota(jnp.int32, (c, c), 1))
    for sub in range(tm // c):
        rows = slice(sub * c, (sub + 1) * c)
        for hh in range(RET_HEADS):
            qk_cols = slice(hh * RET_DK, (hh + 1) * RET_DK)
            v_cols = slice(hh * RET_DV, (hh + 1) * RET_DV)
            q = (_rot_half(q_all[rows, qk_cols], cos[rows], sin[rows]) * xi_ref[hh]).astype(BF16)
            k = (_rot_half(k_all[rows, qk_cols], cos[rows], sin[rows]) * ks_ref[hh]).astype(BF16)
            v = v_all[rows, v_cols]
            state = state_ref[hh]
            scores = jnp.where(causal, _dot_nt(q, k), 0.0).astype(BF16)
            o = _dot(jnp.concatenate([scores, q], axis=1),
                     jnp.concatenate([v, state.astype(BF16)], axis=0))
            state_ref[hh] = chunk_decay[hh] * (state + _dot_tn(k, v))
            o = o * lax.rsqrt(jnp.mean(o * o, axis=-1, keepdims=True) + EPS)
            o_ref[rows, v_cols] = (o * jax.nn.silu(g_all[rows, v_cols])).astype(BF16)
        if gate_dots:
            gate_ref, col = gate_dots.pop(0)
            gate_ref[...] = _dot(z, w_ref[:, col:col + d]).astype(BF16)
    for gate_ref, col in gate_dots:
        gate_ref[...] = _dot(z, w_ref[:, col:col + d]).astype(BF16)


def _in_proj(h, nw, wuT, w, layer, batch, seq):
    n, d = h.shape
    tm = min(IN_PROJ_TILE, seq)
    nt = seq // tm
    cos, sin = _rotary_tables(jnp.arange(tm, dtype=F32))
    cos0, sin0 = _rotary_tables(jnp.arange(nt, dtype=F32) * tm)
    cos0, sin0 = cos0.reshape(nt, 1, LANES), sin0.reshape(nt, 1, LANES)
    sw = wuT.shape[1]
    vw = RET_HEADS * RET_DV
    c = min(RET_CHUNK, tm)
    gamma = 1.0 - 2.0 ** (-5.0 - np.arange(RET_HEADS, dtype=np.float64))
    idx = np.arange(c, dtype=np.float64)
    xi = np.broadcast_to((gamma[:, None] ** (idx + 1.0))[:, :, None], (RET_HEADS, c, RET_DK))
    ks = np.broadcast_to((RET_DK ** -0.5 * gamma[:, None] ** (-1.0 - idx))[:, :, None],
                         (RET_HEADS, c, RET_DK))
    chunk_decay = tuple(float(x) for x in gamma ** c)
    row = lambda wd: pl.BlockSpec((tm, wd), lambda i: (i, 0))
    tab0 = pl.BlockSpec((None, 1, LANES), lambda i: (i % nt, 0, 0))
    slabs = sw // SUBLANES
    rows = seq // LANES * TILE_PITCH
    out_shape = (
        jax.ShapeDtypeStruct((batch, slabs, rows, LANES), F32),
        jax.ShapeDtypeStruct((n, vw), BF16),
        jax.ShapeDtypeStruct((n, d), BF16), jax.ShapeDtypeStruct((n, d), BF16),
    )
    return pl.pallas_call(
        functools.partial(_in_proj_body, chunk_decay=chunk_decay, tiles_per_seq=nt),
        grid=(n // tm,),
        in_specs=[row(d), _layer(nw, layer), _layer(wuT, layer), _layer(w, layer),
                  _whole(cos.shape), _whole(sin.shape), tab0, tab0,
                  _whole(xi.shape), _whole(ks.shape)],
        out_specs=(pl.BlockSpec((None, slabs, tm // LANES * TILE_PITCH, LANES),
                                lambda i: (i // nt, 0, i % nt, 0)),
                   row(vw), row(d), row(d)),
        out_shape=out_shape,
        scratch_shapes=[pltpu.VMEM((RET_HEADS, RET_DK, RET_DV), F32)],
        compiler_params=_params(1),
        name="in_proj",
    )(h, nw, wuT, w, cos, sin, cos0, sin0, jnp.asarray(xi, F32), jnp.asarray(ks, F32))


def _s5_body(u_ref, prow_ref, bx_ref, cx_ref, d_ref, y_ref, taps_ref, m_ref, *, scan_steps):
    slabs_per_group = SSM_GROUP // SUBLANES
    for gg in range(prow_ref.shape[0]):
        _s5_group(u_ref, prow_ref.at[gg], bx_ref.at[gg], cx_ref.at[gg], d_ref.at[gg], y_ref,
                  taps_ref.at[gg], m_ref.at[gg], slab0=gg * slabs_per_group, scan_steps=scan_steps)


def _s5_group(u_ref, prow_ref, bx_ref, cx_ref, d_ref, y_ref, taps_ref, m_ref, *, slab0, scan_steps):
    batch = u_ref.shape[0]
    n = u_ref.shape[2] // TILE_PITCH
    t = S5_CHUNK
    half = LANES // 2
    lane = lax.broadcasted_iota(jnp.int32, (1, LANES), 1)
    lo = lane < half
    sgn = jnp.where(lo, -1.0, 1.0)

    def c_form(x):
        return jnp.where(lo, x, pltpu.roll(x, half, 1))

    def s_form(x):
        return jnp.where(lo, -pltpu.roll(x, half, 1), x)

    lr, li = prow_ref[0:1, :], prow_ref[1:2, :]
    step = jnp.exp(prow_ref[2:3, :])
    mag = jnp.exp(lr * step)
    ar, ai = mag * jnp.cos(li * step), mag * jnp.sin(li * step)
    den = lr * lr + li * li
    nr = ar - 1.0
    kr = (nr * lr + ai * li) / den
    ki = (ai * lr - nr * li) / den
    squares = [(ar, ai * sgn)]
    for _ in range(S5_CHUNK_BITS + scan_steps):
        c, s = squares[-1]
        squares.append((c * c - s * s, 2.0 * c * s))

    def powers(expo):
        x = jnp.broadcast_to(jnp.where(lo, 1.0, 0.0), (t, LANES))
        for j in range(S5_CHUNK_BITS):
            x = jnp.where(((expo >> j) & 1) == 1, _rot_half(x, *squares[j]), x)
        return x

    row_id = lax.broadcasted_iota(jnp.int32, (t, LANES), 0)
    a_tau = powers(row_id)
    a_rev = powers(t - 1 - row_id)
    a_next = _rot_half(a_tau, *squares[0])

    bbar = _rot_half(bx_ref[...], kr, ki * sgn)
    bb_c, bb_s = c_form(bbar), s_form(bbar)
    cmat = cx_ref[...]
    cc_c, cc_s = c_form(cmat), s_form(cmat)

    g1 = jnp.concatenate(
        [_rot_half(cmat, bb_c[ci:ci + 1], bb_s[ci:ci + 1]) * (-sgn) for ci in range(SSM_GROUP)], axis=0)
    taps_ref[...] = _dot_nt(g1, a_tau, precision=lax.Precision.HIGHEST)
    bs = jnp.concatenate(
        [_rot_half(a_rev, bb_c[ci:ci + 1], bb_s[ci:ci + 1]) for ci in range(SSM_GROUP)],
        axis=0).astype(BF16)
    cxt = jnp.concatenate(
        [_rot_half(a_next, cc_c[co:co + 1], cc_s[co:co + 1]) * (-sgn) for co in range(SSM_GROUP)],
        axis=0).astype(BF16)

    def channel(ref, b, c):
        return ref.at[b, slab0 + c // SUBLANES, pl.ds(c % SUBLANES, n, stride=TILE_PITCH), :]

    u = jnp.concatenate(
        [jnp.concatenate([channel(u_ref, b, c)[...].astype(BF16) for c in range(SSM_GROUP)], axis=1)
         for b in range(batch)], axis=0)
    x = _dot(u, bs)
    rows = lax.broadcasted_iota(jnp.int32, x.shape, 0) % n
    for k in range(scan_steps):
        sh = 1 << k
        prev = jnp.where(rows >= sh, pltpu.roll(x, sh, 0), 0.0)
        x = x + _rot_half(prev, *squares[S5_CHUNK_BITS + k])
    h_prev = jnp.where(rows >= 1, pltpu.roll(x, 1, 0), 0.0).astype(BF16)

    causal = (lax.broadcasted_iota(jnp.int32, (t, t), 1)
              >= lax.broadcasted_iota(jnp.int32, (t, t), 0))
    pair = 2 * t
    for b in range(batch):
        for slab in range(slab0, slab0 + SSM_GROUP // SUBLANES):
            for pad in range(SUBLANES, TILE_PITCH):
                y_ref[b, slab, pl.ds(pad, n, stride=TILE_PITCH), :] = jnp.zeros((n, LANES), F32)
    for jp in range(SSM_GROUP // 2):
        for ci in range(SSM_GROUP):
            for co in range(2 * jp, 2 * jp + 2):
                tap_row = taps_ref[ci * SSM_GROUP + co:ci * SSM_GROUP + co + 1, :]
                blk = pltpu.roll(jnp.broadcast_to(tap_row, (t, t)), 0, 1, stride=1, stride_axis=0)
                m_ref[ci * t:(ci + 1) * t, co * t:(co + 1) * t] = (
                    jnp.where(causal, blk, 0.0).astype(BF16))
        y2 = (_dot(u, m_ref[:, jp * pair:(jp + 1) * pair])
              + _dot_nt(h_prev, cxt[jp * pair:(jp + 1) * pair, :]))
        for co in range(2 * jp, 2 * jp + 2):
            for b in range(batch):
                skip = d_ref[co:co + 1, :] * channel(u_ref, b, co)[...]
                y = y2[b * n:(b + 1) * n, (co - 2 * jp) * t:(co - 2 * jp + 1) * t] + skip
                channel(y_ref, b, co)[...] = jax.nn.gelu(y)


def _s5(u4, prow, bx, cx, dB):
    batch, slabs, rows, _ = u4.shape
    groups = prow.shape[0]
    n_chunks = rows // TILE_PITCH
    scan_steps = max(n_chunks - 1, 0).bit_length()
    width = SSM_GROUP * S5_CHUNK
    gps = S5_GROUPS_PER_STEP
    per = slabs // groups * gps
    grp = lambda *shape: pl.BlockSpec((gps,) + shape, lambda g: (g,) + (0,) * len(shape))
    io = pl.BlockSpec((batch, per, rows, LANES), lambda g: (0, g, 0, 0))
    return pl.pallas_call(
        functools.partial(_s5_body, scan_steps=scan_steps),
        grid=(groups // gps,),
        in_specs=[io, grp(3, LANES), grp(SSM_GROUP, LANES), grp(SSM_GROUP, LANES),
                  grp(SSM_GROUP, LANES)],
        out_specs=io,
        out_shape=jax.ShapeDtypeStruct(u4.shape, F32),
        scratch_shapes=[pltpu.VMEM((gps, SSM_GROUP * SSM_GROUP, S5_CHUNK), F32),
                        pltpu.VMEM((gps, width, width), BF16)],
        compiler_params=_params(1),
        name="s5",
    )(u4, prow, bx, cx, dB)


def _mix_ffn_body(h_ref, y_ref, yb_ref, ga_ref, gb_ref, wgT_ref, bg_ref, wpa_ref, wpb_ref,
                  wo_ref, nm_ref, wup_ref, wdn_ref, nf_ref, o_ref, *, final):
    slabs = y_ref.shape[0]
    chunks = y_ref.shape[1] // TILE_PITCH
    yT = jnp.concatenate(
        [jnp.concatenate([y_ref[r, j * TILE_PITCH:j * TILE_PITCH + SUBLANES, :] for j in range(chunks)],
                         axis=1)
         for r in range(slabs)], axis=0)
    gate = jax.nn.sigmoid(_dot(wgT_ref[...], yT.astype(BF16)) + bg_ref[...])
    yaT = (yT * gate).astype(BF16)
    merged = (jax.nn.sigmoid(ga_ref[...].astype(F32)) * _dot_tn(yaT, wpa_ref[...])
              + jax.nn.sigmoid(gb_ref[...].astype(F32)) * _dot(yb_ref[...], wpb_ref[...]))
    h = h_ref[...] + _dot(merged.astype(BF16), wo_ref[...])
    z = _rmsnorm(h, nm_ref[...]).astype(BF16)
    ff = wup_ref.shape[1]
    step = 1024
    for j in range(0, ff, step):
        a = jnp.square(jnp.maximum(_dot(z, wup_ref[:, j:j + step]), 0.0)).astype(BF16)
        h = h + _dot(a, wdn_ref[j:j + step, :])
    if final:
        h = _rmsnorm(h, nf_ref[...])
    o_ref[...] = h


def _mix_ffn(h, y4, yb, ga, gb, wgT, bg, wpa, wpb, wo, nm, wup, wdn, nf, layer, batch, seq, final):
    n, d = h.shape
    tm = min(MIX_FFN_TILE, seq)
    nt = seq // tm
    slabs = y4.shape[1]
    row = lambda w: pl.BlockSpec((tm, w), lambda i: (i, 0))
    stacks = (wgT, bg, wpa, wpb, wo, nm, wup, wdn)
    return pl.pallas_call(
        functools.partial(_mix_ffn_body, final=final),
        grid=(n // tm,),
        in_specs=[row(d),
                  pl.BlockSpec((None, slabs, tm // LANES * TILE_PITCH, LANES),
                               lambda i: (i // nt, 0, i % nt, 0)),
                  row(yb.shape[1]), row(d), row(d)]
        + [_layer(w, layer) for w in stacks] + [_whole(nf.shape)],
        out_specs=row(d),
        out_shape=jax.ShapeDtypeStruct((n, d), F32),
        compiler_params=_params(1),
        name="mix_ffn",
    )(h, y4, yb, ga, gb, *stacks, nf)


def _rotary_tables(pos):
    inv_freq = ROPE_BASE ** (-jnp.arange(0, RET_DK, 2, dtype=F32) / RET_DK)
    ang = pos[:, None] * inv_freq[None, :]
    cos, sin = jnp.cos(ang), jnp.sin(ang)
    return jnp.concatenate([cos, cos], axis=1), jnp.concatenate([-sin, sin], axis=1)


def _dup(x):
    return jnp.concatenate([x, x], axis=-1)


def kernel(x, w_in, lam_re, lam_im, b_re, b_im, c_re, c_im, d_skip, log_step, w_glu, b_glu,
           w_proj_ssm, w_proj_ret, w_out, norm_mix, norm_mlp, w_up, w_down, norm_final):
    batch, seq, d = x.shape
    depth, groups, p = lam_re.shape
    sw = w_glu.shape[1]
    assert p == SSM_STATE and sw == groups * SSM_GROUP and 2 * p == LANES
    assert seq % S5_CHUNK == 0
    assert seq % min(IN_PROJ_TILE, seq) == 0 and seq % min(MIX_FFN_TILE, seq) == 0
    assert min(IN_PROJ_TILE, seq) % min(RET_CHUNK, seq) == 0
    h = x.reshape(batch * seq, d).astype(F32)
    w_all = w_in.astype(BF16)
    wuT_all = jnp.swapaxes(w_all[:, :, :sw], 1, 2)
    wgT_all = jnp.swapaxes(w_glu, 1, 2).astype(BF16)
    wpa_all, wpb_all, wo_all = (w.astype(BF16) for w in (w_proj_ssm, w_proj_ret, w_out))
    wup_all, wdn_all = w_up.astype(BF16), w_down.astype(BF16)
    nmix_all = norm_mix.reshape(depth, 1, d).astype(F32)
    nmlp_all = norm_mlp.reshape(depth, 1, d).astype(F32)
    bg_all = b_glu.reshape(depth, sw, 1).astype(F32)
    for l in range(depth):
        u4, yb, ga, gb = _in_proj(h, nmix_all, wuT_all, w_all, l, batch, seq)
        prow = jnp.stack([_dup(lam_re[l]), _dup(lam_im[l]),
                          jnp.broadcast_to(log_step[l][:, None], (groups, LANES))], axis=1).astype(F32)
        bx = jnp.concatenate([b_re[l], b_im[l]], axis=1).transpose(0, 2, 1).astype(F32)
        cx = jnp.concatenate([c_re[l], c_im[l]], axis=2).astype(F32)
        dB = jnp.broadcast_to(d_skip[l].reshape(groups, SSM_GROUP, 1), (groups, SSM_GROUP, LANES))
        y4 = _s5(u4, prow, bx, cx, dB.astype(F32))
        h = _mix_ffn(
            h, y4, yb, ga, gb, wgT_all, bg_all, wpa_all, wpb_all, wo_all, nmlp_all, wup_all, wdn_all,
            norm_final.reshape(1, d).astype(F32), l, batch, seq, final=(l == depth - 1))
    return h.reshape(batch, seq, d).astype(x.dtype)
```

```python
import functools

import numpy as np
import jax
import jax.numpy as jnp
from jax import lax
from jax.experimental import pallas as pl
from jax.experimental.pallas import tpu as pltpu

F32 = jnp.float32
BF16 = jnp.bfloat16

EPS = 1e-6
SSM_GROUP = 16
SSM_STATE = 64
RET_HEADS = 4
RET_DK = 128
RET_DV = 256
ROPE_BASE = 10000.0

LANES = 128
SUBLANES = 8
TILE_PITCH = 10
S5_CHUNK = LANES
S5_CHUNK_BITS = 7
S5_GROUPS_PER_STEP = 2
RET_CHUNK = 256
IN_PROJ_TILE = 1024
MIX_TILE = 1024
FFN_TILE = 1024
VMEM_LIMIT = 56 * 1024 * 1024


def _params(n_axes, vmem=VMEM_LIMIT):
    return pltpu.CompilerParams(
        dimension_semantics=("arbitrary",) * n_axes, vmem_limit_bytes=vmem)


def _whole(shape):
    zeros = (0,) * len(shape)
    return pl.BlockSpec(shape, lambda *_: zeros, pipeline_mode=pl.Buffered(1))


def _layer(stack, l):
    zeros = (0,) * (stack.ndim - 1)
    return pl.BlockSpec((None,) + stack.shape[1:], lambda *_: (l,) + zeros,
                        pipeline_mode=pl.Buffered(1))


def _dot(a, b):
    return jnp.dot(a, b, preferred_element_type=F32)


def _dot_nt(a, b, precision=None):
    return lax.dot_general(a, b, (((1,), (1,)), ((), ())), precision=precision,
                           preferred_element_type=F32)


def _dot_tn(a, b):
    return lax.dot_general(a, b, (((0,), (0,)), ((), ())), preferred_element_type=F32)


def _rmsnorm(x, w):
    return x * lax.rsqrt(jnp.mean(x * x, axis=-1, keepdims=True) + EPS) * w


def _rot_half(x, c, s):
    return x * c + pltpu.roll(x, LANES // 2, 1) * s


def _in_proj_body(h_ref, nw_ref, wuT_ref, w_ref, cos_ref, sin_ref, cos0_ref, sin0_ref, xi_ref, ks_ref,
                  u_ref, o_ref, ga_ref, gb_ref, state_ref, *, chunk_decay, tiles_per_seq):
    @pl.when(pl.program_id(0) % tiles_per_seq == 0)
    def _():
        state_ref[...] = jnp.zeros_like(state_ref)

    z = _rmsnorm(h_ref[...], nw_ref[...]).astype(BF16)
    uT = _dot_nt(wuT_ref[...], z)
    chunks = uT.shape[1] // LANES
    for r in range(uT.shape[0] // SUBLANES):
        for pad in range(SUBLANES, TILE_PITCH):
            u_ref[r, pl.ds(pad, chunks, stride=TILE_PITCH), :] = jnp.zeros((chunks, LANES), F32)
        for j in range(chunks):
            u_ref[r, j * TILE_PITCH:j * TILE_PITCH + SUBLANES, :] = (
                uT[r * SUBLANES:(r + 1) * SUBLANES, j * LANES:(j + 1) * LANES])
    cos = cos_ref[...] * cos0_ref[...] - sin_ref[...] * sin0_ref[...]
    sin = sin_ref[...] * cos0_ref[...] + cos_ref[...] * sin0_ref[...]
    base = uT.shape[0]
    qk = RET_HEADS * RET_DK
    vw = RET_HEADS * RET_DV
    d = ga_ref.shape[1]
    q_all = _dot(z, w_ref[:, base:base + qk])
    k_all = _dot(z, w_ref[:, base + qk:base + 2 * qk])
    base += 2 * qk
    v_all = _dot(z, w_ref[:, base:base + vw]).astype(BF16)
    g_all = _dot(z, w_ref[:, base + vw:base + 2 * vw])
    base += 2 * vw
    gate_dots = [(ga_ref, base), (gb_ref, base + d)]

    c = xi_ref.shape[1]
    tm = z.shape[0]
    causal = (lax.broadcasted_iota(jnp.int32, (c, c), 0)
              >= lax.broadcasted_iota(jnp.int32, (c, c), 1))
    for sub in range(tm // c):
        rows = slice(sub * c, (sub + 1) * c)
        for hh in range(RET_HEADS):
            qk_cols = slice(hh * RET_DK, (hh + 1) * RET_DK)
            v_cols = slice(hh * RET_DV, (hh + 1) * RET_DV)
            q = (_rot_half(q_all[rows, qk_cols], cos[rows], sin[rows]) * xi_ref[hh]).astype(BF16)
            k = (_rot_half(k_all[rows, qk_cols], cos[rows], sin[rows]) * ks_ref[hh]).astype(BF16)
            v = v_all[rows, v_cols]
            state = state_ref[hh]
            scores = jnp.where(causal, _dot_nt(q, k), 0.0).astype(BF16)
            o = _dot(jnp.concatenate([scores, q], axis=1),
                     jnp.concatenate([v, state.astype(BF16)], axis=0))
            state_ref[hh] = chunk_decay[hh] * (state + _dot_tn(k, v))
            o = o * lax.rsqrt(jnp.mean(o * o, axis=-1, keepdims=True) + EPS)
            o_ref[rows, v_cols] = (o * jax.nn.silu(g_all[rows, v_cols])).astype(BF16)
        if gate_dots:
            gate_ref, col = gate_dots.pop(0)
            gate_ref[...] = _dot(z, w_ref[:, col:col + d]).astype(BF16)
    for gate_ref, col in gate_dots:
        gate_ref[...] = _dot(z, w_ref[:, col:col + d]).astype(BF16)


def _in_proj(h, nw, wuT, w, layer, batch, seq):
    n, d = h.shape
    tm = min(IN_PROJ_TILE, seq)
    nt = seq // tm
    cos, sin = _rotary_tables(jnp.arange(tm, dtype=F32))
    cos0, sin0 = _rotary_tables(jnp.arange(nt, dtype=F32) * tm)
    cos0, sin0 = cos0.reshape(nt, 1, LANES), sin0.reshape(nt, 1, LANES)
    sw = wuT.shape[1]
    vw = RET_HEADS * RET_DV
    c = min(RET_CHUNK, tm)
    gamma = 1.0 - 2.0 ** (-5.0 - np.arange(RET_HEADS, dtype=np.float64))
    idx = np.arange(c, dtype=np.float64)
    xi = np.broadcast_to((gamma[:, None] ** (idx + 1.0))[:, :, None], (RET_HEADS, c, RET_DK))
    ks = np.broadcast_to((RET_DK ** -0.5 * gamma[:, None] ** (-1.0 - idx))[:, :, None],
                         (RET_HEADS, c, RET_DK))
    chunk_decay = tuple(float(x) for x in gamma ** c)
    row = lambda wd: pl.BlockSpec((tm, wd), lambda i: (i, 0))
    tab0 = pl.BlockSpec((None, 1, LANES), lambda i: (i % nt, 0, 0))
    slabs = sw // SUBLANES
    rows = seq // LANES * TILE_PITCH
    out_shape = (
        jax.ShapeDtypeStruct((batch, slabs, rows, LANES), F32),
        jax.ShapeDtypeStruct((n, vw), BF16),
        jax.ShapeDtypeStruct((n, d), BF16), jax.ShapeDtypeStruct((n, d), BF16),
    )
    return pl.pallas_call(
        functools.partial(_in_proj_body, chunk_decay=chunk_decay, tiles_per_seq=nt),
        grid=(n // tm,),
        in_specs=[row(d), _layer(nw, layer), _layer(wuT, layer), _layer(w, layer),
                  _whole(cos.shape), _whole(sin.shape), tab0, tab0,
                  _whole(xi.shape), _whole(ks.shape)],
        out_specs=(pl.BlockSpec((None, slabs, tm // LANES * TILE_PITCH, LANES),
                                lambda i: (i // nt, 0, i % nt, 0)),
                   row(vw), row(d), row(d)),
        out_shape=out_shape,
        scratch_shapes=[pltpu.VMEM((RET_HEADS, RET_DK, RET_DV), F32)],
        compiler_params=_params(1),
        name="in_proj",
    )(h, nw, wuT, w, cos, sin, cos0, sin0, jnp.asarray(xi, F32), jnp.asarray(ks, F32))


def _s5_body(u_ref, prow_ref, bx_ref, cx_ref, d_ref, y_ref, taps_ref, m_ref, *, scan_steps):
    slabs_per_group = SSM_GROUP // SUBLANES
    for gg in range(prow_ref.shape[0]):
        _s5_group(u_ref, prow_ref.at[gg], bx_ref.at[gg], cx_ref.at[gg], d_ref.at[gg], y_ref,
                  taps_ref.at[gg], m_ref.at[gg], slab0=gg * slabs_per_group, scan_steps=scan_steps)


def _s5_group(u_ref, prow_ref, bx_ref, cx_ref, d_ref, y_ref, taps_ref, m_ref, *, slab0, scan_steps):
    batch = u_ref.shape[0]
    n = u_ref.shape[2] // TILE_PITCH
    t = S5_CHUNK
    half = LANES // 2
    lane = lax.broadcasted_iota(jnp.int32, (1, LANES), 1)
    lo = lane < half
    sgn = jnp.where(lo, -1.0, 1.0)

    def c_form(x):
        return jnp.where(lo, x, pltpu.roll(x, half, 1))

    def s_form(x):
        return jnp.where(lo, -pltpu.roll(x, half, 1), x)

    lr, li = prow_ref[0:1, :], prow_ref[1:2, :]
    step = jnp.exp(prow_ref[2:3, :])
    mag = jnp.exp(lr * step)
    ar, ai = mag * jnp.cos(li * step), mag * jnp.sin(li * step)
    den = lr * lr + li * li
    nr = ar - 1.0
    kr = (nr * lr + ai * li) / den
    ki = (ai * lr - nr * li) / den
    squares = [(ar, ai * sgn)]
    for _ in range(S5_CHUNK_BITS + scan_steps):
        c, s = squares[-1]
        squares.append((c * c - s * s, 2.0 * c * s))

    def powers(expo):
        x = jnp.broadcast_to(jnp.where(lo, 1.0, 0.0), (t, LANES))
        for j in range(S5_CHUNK_BITS):
            x = jnp.where(((expo >> j) & 1) == 1, _rot_half(x, *squares[j]), x)
        return x

    row_id = lax.broadcasted_iota(jnp.int32, (t, LANES), 0)
    a_tau = powers(row_id)
    a_rev = powers(t - 1 - row_id)
    a_next = _rot_half(a_tau, *squares[0])

    bbar = _rot_half(bx_ref[...], kr, ki * sgn)
    bb_c, bb_s = c_form(bbar), s_form(bbar)
    cmat = cx_ref[...]
    cc_c, cc_s = c_form(cmat), s_form(cmat)

    g1 = jnp.concatenate(
        [_rot_half(cmat, bb_c[ci:ci + 1], bb_s[ci:ci + 1]) * (-sgn) for ci in range(SSM_GROUP)], axis=0)
    taps_ref[...] = _dot_nt(g1, a_tau, precision=lax.Precision.HIGHEST)
    bs = jnp.concatenate(
        [_rot_half(a_rev, bb_c[ci:ci + 1], bb_s[ci:ci + 1]) for ci in range(SSM_GROUP)],
        axis=0).astype(BF16)
    cxt = jnp.concatenate(
        [_rot_half(a_next, cc_c[co:co + 1], cc_s[co:co + 1]) * (-sgn) for co in range(SSM_GROUP)],
        axis=0).astype(BF16)

    def channel(ref, b, c):
        return ref.at[b, slab0 + c // SUBLANES, pl.ds(c % SUBLANES, n, stride=TILE_PITCH), :]

    u = jnp.concatenate(
        [jnp.concatenate([channel(u_ref, b, c)[...].astype(BF16) for c in range(SSM_GROUP)], axis=1)
         for b in range(batch)], axis=0)
    x = _dot(u, bs)
    rows = lax.broadcasted_iota(jnp.int32, x.shape, 0) % n
    for k in range(scan_steps):
        sh = 1 << k
        prev = jnp.where(rows >= sh, pltpu.roll(x, sh, 0), 0.0)
        x = x + _rot_half(prev, *squares[S5_CHUNK_BITS + k])
    h_prev = jnp.where(rows >= 1, pltpu.roll(x, 1, 0), 0.0).astype(BF16)

    causal = (lax.broadcasted_iota(jnp.int32, (t, t), 1)
              >= lax.broadcasted_iota(jnp.int32, (t, t), 0))
    pair = 2 * t
    for b in range(batch):
        for slab in range(slab0, slab0 + SSM_GROUP // SUBLANES):
            for pad in range(SUBLANES, TILE_PITCH):
                y_ref[b, slab, pl.ds(pad, n, stride=TILE_PITCH), :] = jnp.zeros((n, LANES), F32)
    for jp in range(SSM_GROUP // 2):
        for ci in range(SSM_GROUP):
            for co in range(2 * jp, 2 * jp + 2):
                tap_row = taps_ref[ci * SSM_GROUP + co:ci * SSM_GROUP + co + 1, :]
                blk = pltpu.roll(jnp.broadcast_to(tap_row, (t, t)), 0, 1, stride=1, stride_axis=0)
                m_ref[ci * t:(ci + 1) * t, co * t:(co + 1) * t] = (
                    jnp.where(causal, blk, 0.0).astype(BF16))
        y2 = (_dot(u, m_ref[:, jp * pair:(jp + 1) * pair])
              + _dot_nt(h_prev, cxt[jp * pair:(jp + 1) * pair, :]))
        for co in range(2 * jp, 2 * jp + 2):
            for b in range(batch):
                skip = d_ref[co:co + 1, :] * channel(u_ref, b, co)[...]
                y = y2[b * n:(b + 1) * n, (co - 2 * jp) * t:(co - 2 * jp + 1) * t] + skip
                channel(y_ref, b, co)[...] = jax.nn.gelu(y)


def _s5(u4, prow, bx, cx, dB):
    batch, slabs, rows, _ = u4.shape
    groups = prow.shape[0]
    n_chunks = rows // TILE_PITCH
    scan_steps = max(n_chunks - 1, 0).bit_length()
    width = SSM_GROUP * S5_CHUNK
    gps = S5_GROUPS_PER_STEP
    per = slabs // groups * gps
    grp = lambda *shape: pl.BlockSpec((gps,) + shape, lambda g: (g,) + (0,) * len(shape))
    io = pl.BlockSpec((batch, per, rows, LANES), lambda g: (0, g, 0, 0))
    return pl.pallas_call(
        functools.partial(_s5_body, scan_steps=scan_steps),
        grid=(groups // gps,),
        in_specs=[io, grp(3, LANES), grp(SSM_GROUP, LANES), grp(SSM_GROUP, LANES),
                  grp(SSM_GROUP, LANES)],
        out_specs=io,
        out_shape=jax.ShapeDtypeStruct(u4.shape, F32),
        scratch_shapes=[pltpu.VMEM((gps, SSM_GROUP * SSM_GROUP, S5_CHUNK), F32),
                        pltpu.VMEM((gps, width, width), BF16)],
        compiler_params=_params(1),
        name="s5",
    )(u4, prow, bx, cx, dB)


def _mix_body(h_ref, y_ref, yb_ref, ga_ref, gb_ref, wgT_ref, bg_ref, wpa_ref, wpb_ref, wo_ref, o_ref):
    slabs = y_ref.shape[0]
    chunks = y_ref.shape[1] // TILE_PITCH
    yT = jnp.concatenate(
        [jnp.concatenate([y_ref[r, j * TILE_PITCH:j * TILE_PITCH + SUBLANES, :] for j in range(chunks)],
                         axis=1)
         for r in range(slabs)], axis=0)
    gate = jax.nn.sigmoid(_dot(wgT_ref[...], yT.astype(BF16)) + bg_ref[...])
    yaT = (yT * gate).astype(BF16)
    merged = (jax.nn.sigmoid(ga_ref[...].astype(F32)) * _dot_tn(yaT, wpa_ref[...])
              + jax.nn.sigmoid(gb_ref[...].astype(F32)) * _dot(yb_ref[...], wpb_ref[...]))
    o_ref[...] = h_ref[...] + _dot(merged.astype(BF16), wo_ref[...])


def _mix(h, y4, yb, ga, gb, wgT, bg, wpa, wpb, wo, layer, batch, seq):
    n, d = h.shape
    tm = min(MIX_TILE, seq)
    nt = seq // tm
    slabs = y4.shape[1]
    row = lambda w: pl.BlockSpec((tm, w), lambda i: (i, 0))
    stacks = (wgT, bg, wpa, wpb, wo)
    return pl.pallas_call(
        _mix_body,
        grid=(n // tm,),
        in_specs=[row(d),
                  pl.BlockSpec((None, slabs, tm // LANES * TILE_PITCH, LANES),
                               lambda i: (i // nt, 0, i % nt, 0)),
                  row(yb.shape[1]), row(d), row(d)]
        + [_layer(w, layer) for w in stacks],
        out_specs=row(d),
        out_shape=jax.ShapeDtypeStruct((n, d), F32),
        compiler_params=_params(1),
        name="mix",
    )(h, y4, yb, ga, gb, *stacks)


def _ffn_body(h_ref, nm_ref, wup_ref, wdn_ref, nf_ref, o_ref, *, final):
    h = h_ref[...]
    z = _rmsnorm(h, nm_ref[...]).astype(BF16)
    ff = wup_ref.shape[1]
    step = 1024
    for j in range(0, ff, step):
        a = jnp.square(jnp.maximum(_dot(z, wup_ref[:, j:j + step]), 0.0)).astype(BF16)
        h = h + _dot(a, wdn_ref[j:j + step, :])
    if final:
        h = _rmsnorm(h, nf_ref[...])
    o_ref[...] = h


def _ffn(h, nm, wup, wdn, nf, layer, seq, final):
    n, d = h.shape
    tm = min(FFN_TILE, seq)
    row = pl.BlockSpec((tm, d), lambda i: (i, 0))
    return pl.pallas_call(
        functools.partial(_ffn_body, final=final),
        grid=(n // tm,),
        in_specs=[row, _layer(nm, layer), _layer(wup, layer), _layer(wdn, layer), _whole(nf.shape)],
        out_specs=row,
        out_shape=jax.ShapeDtypeStruct((n, d), F32),
        compiler_params=_params(1),
        name="ffn",
    )(h, nm, wup, wdn, nf)


def _rotary_tables(pos):
    inv_freq = ROPE_BASE ** (-jnp.arange(0, RET_DK, 2, dtype=F32) / RET_DK)
    ang = pos[:, None] * inv_freq[None, :]
    cos, sin = jnp.cos(ang), jnp.sin(ang)
    return jnp.concatenate([cos, cos], axis=1), jnp.concatenate([-sin, sin], axis=1)


def _dup(x):
    return jnp.concatenate([x, x], axis=-1)


def kernel(x, w_in, lam_re, lam_im, b_re, b_im, c_re, c_im, d_skip, log_step, w_glu, b_glu,
           w_proj_ssm, w_proj_ret, w_out, norm_mix, norm_mlp, w_up, w_down, norm_final):
    batch, seq, d = x.shape
    depth, groups, p = lam_re.shape
    sw = w_glu.shape[1]
    assert p == SSM_STATE and sw == groups * SSM_GROUP and 2 * p == LANES
    assert seq % S5_CHUNK == 0
    assert seq % min(IN_PROJ_TILE, seq) == 0 and seq % min(MIX_TILE, seq) == 0 and seq % min(FFN_TILE, seq) == 0
    assert min(IN_PROJ_TILE, seq) % min(RET_CHUNK, seq) == 0
    h = x.reshape(batch * seq, d).astype(F32)
    w_all = w_in.astype(BF16)
    wuT_all = jnp.swapaxes(w_all[:, :, :sw], 1, 2)
    wgT_all = jnp.swapaxes(w_glu, 1, 2).astype(BF16)
    wpa_all, wpb_all, wo_all = (w.astype(BF16) for w in (w_proj_ssm, w_proj_ret, w_out))
    wup_all, wdn_all = w_up.astype(BF16), w_down.astype(BF16)
    nmix_all = norm_mix.reshape(depth, 1, d).astype(F32)
    nmlp_all = norm_mlp.reshape(depth, 1, d).astype(F32)
    bg_all = b_glu.reshape(depth, sw, 1).astype(F32)
    for l in range(depth):
        u4, yb, ga, gb = _in_proj(h, nmix_all, wuT_all, w_all, l, batch, seq)
        prow = jnp.stack([_dup(lam_re[l]), _dup(lam_im[l]),
                          jnp.broadcast_to(log_step[l][:, None], (groups, LANES))], axis=1).astype(F32)
        bx = jnp.concatenate([b_re[l], b_im[l]], axis=1).transpose(0, 2, 1).astype(F32)
        cx = jnp.concatenate([c_re[l], c_im[l]], axis=2).astype(F32)
        dB = jnp.broadcast_to(d_skip[l].reshape(groups, SSM_GROUP, 1), (groups, SSM_GROUP, LANES))
        y4 = _s5(u4, prow, bx, cx, dB.astype(F32))
        h = _mix(h, y4, yb, ga, gb, wgT_all, bg_all, wpa_all, wpb_all, wo_all, l, batch, seq)
        h = _ffn(h, nmlp_all, wup_all, wdn_all, norm_final.reshape(1, d).astype(F32), l, seq,
                 final=(l == depth - 1))
    return h.reshape(batch, seq, d).astype(x.dtype)
```

```python
import functools

import numpy as np
import jax
import jax.numpy as jnp
from jax import lax
from jax.experimental import pallas as pl
from jax.experimental.pallas import tpu as pltpu

F32 = jnp.float32
BF16 = jnp.bfloat16

EPS = 1e-6
SSM_GROUP = 16
SSM_STATE = 64
RET_HEADS = 4
RET_DK = 128
RET_DV = 256
ROPE_BASE = 10000.0

LANES = 128
SUBLANES = 8
TILE_PITCH = 10
S5_CHUNK = LANES
S5_CHUNK_BITS = 7
S5_GROUPS_PER_STEP = 2
RET_CHUNK = 256
IN_PROJ_TILE = 1024
MIX_TILE = 1024
FFN_TILE = 1024
VMEM_LIMIT = 56 * 1024 * 1024


def _params(n_axes, vmem=VMEM_LIMIT):
    return pltpu.CompilerParams(
        dimension_semantics=("arbitrary",) * n_axes, vmem_limit_bytes=vmem)


def _whole(shape):
    zeros = (0,) * len(shape)
    return pl.BlockSpec(shape, lambda *_: zeros, pipeline_mode=pl.Buffered(1))


def _layer(stack, l):
    zeros = (0,) * (stack.ndim - 1)
    return pl.BlockSpec((None,) + stack.shape[1:], lambda *_: (l,) + zeros,
                        pipeline_mode=pl.Buffered(1))


def _dot(a, b):
    return jnp.dot(a, b, preferred_element_type=F32)


def _dot_nt(a, b, precision=None):
    return lax.dot_general(a, b, (((1,), (1,)), ((), ())), precision=precision,
                           preferred_element_type=F32)


def _dot_tn(a, b):
    return lax.dot_general(a, b, (((0,), (0,)), ((), ())), preferred_element_type=F32)


def _rmsnorm(x, w):
    return x * lax.rsqrt(jnp.mean(x * x, axis=-1, keepdims=True) + EPS) * w


def _rot_half(x, c, s):
    return x * c + pltpu.roll(x, LANES // 2, 1) * s


def _in_proj_body(h_ref, nw_ref, wuT_ref, w_ref, cos_ref, sin_ref, cos0_ref, sin0_ref, xi_ref, ks_ref,
                  u_ref, o_ref, ga_ref, gb_ref, state_ref, *, chunk_decay, tiles_per_seq):
    @pl.when(pl.program_id(0) % tiles_per_seq == 0)
    def _():
        state_ref[...] = jnp.zeros_like(state_ref)

    z = _rmsnorm(h_ref[...], nw_ref[...]).astype(BF16)
    uT = _dot_nt(wuT_ref[...], z)
    chunks = uT.shape[1] // LANES
    for r in range(uT.shape[0] // SUBLANES):
        for pad in range(SUBLANES, TILE_PITCH):
            u_ref[r, pl.ds(pad, chunks, stride=TILE_PITCH), :] = jnp.zeros((chunks, LANES), F32)
        for j in range(chunks):
            u_ref[r, j * TILE_PITCH:j * TILE_PITCH + SUBLANES, :] = (
                uT[r * SUBLANES:(r + 1) * SUBLANES, j * LANES:(j + 1) * LANES])
    cos = cos_ref[...] * cos0_ref[...] - sin_ref[...] * sin0_ref[...]
    sin = sin_ref[...] * cos0_ref[...] + cos_ref[...] * sin0_ref[...]
    base = uT.shape[0]
    qk = RET_HEADS * RET_DK
    vw = RET_HEADS * RET_DV
    d = ga_ref.shape[1]
    q_all = _dot(z, w_ref[:, base:base + qk])
    k_all = _dot(z, w_ref[:, base + qk:base + 2 * qk])
    base += 2 * qk
    v_all = _dot(z, w_ref[:, base:base + vw]).astype(BF16)
    g_all = _dot(z, w_ref[:, base + vw:base + 2 * vw])
    base += 2 * vw
    gate_dots = [(ga_ref, base), (gb_ref, base + d)]

    c = xi_ref.shape[1]
    tm = z.shape[0]
    causal = (lax.broadcasted_iota(jnp.int32, (c, c), 0)
              >= lax.broadcasted_iota(jnp.int32, (c, c), 1))
    for sub in range(tm // c):
        rows = slice(sub * c, (sub + 1) * c)
        for hh in range(RET_HEADS):
            qk_cols = slice(hh * RET_DK, (hh + 1) * RET_DK)
            v_cols = slice(hh * RET_DV, (hh + 1) * RET_DV)
            q = (_rot_half(q_all[rows, qk_cols], cos[rows], sin[rows]) * xi_ref[hh]).astype(BF16)
            k = (_rot_half(k_all[rows, qk_cols], cos[rows], sin[rows]) * ks_ref[hh]).astype(BF16)
            v = v_all[rows, v_cols]
            state = state_ref[hh]
            scores = jnp.where(causal, _dot_nt(q, k), 0.0).astype(BF16)
            o = _dot(scores, v) + _dot(q, state.astype(BF16))
            state_ref[hh] = chunk_decay[hh] * (state + _dot_tn(k, v))
            o = o * lax.rsqrt(jnp.mean(o * o, axis=-1, keepdims=True) + EPS)
            o_ref[rows, v_cols] = (o * jax.nn.silu(g_all[rows, v_cols])).astype(BF16)
        if gate_dots:
            gate_ref, col = gate_dots.pop(0)
            gate_ref[...] = _dot(z, w_ref[:, col:col + d]).astype(BF16)
    for gate_ref, col in gate_dots:
        gate_ref[...] = _dot(z, w_ref[:, col:col + d]).astype(BF16)


def _in_proj(h, nw, wuT, w, layer, batch, seq):
    n, d = h.shape
    tm = min(IN_PROJ_TILE, seq)
    nt = seq // tm
    cos, sin = _rotary_tables(jnp.arange(tm, dtype=F32))
    cos0, sin0 = _rotary_tables(jnp.arange(nt, dtype=F32) * tm)
    cos0, sin0 = cos0.reshape(nt, 1, LANES), sin0.reshape(nt, 1, LANES)
    sw = wuT.shape[1]
    vw = RET_HEADS * RET_DV
    c = min(RET_CHUNK, tm)
    gamma = 1.0 - 2.0 ** (-5.0 - np.arange(RET_HEADS, dtype=np.float64))
    idx = np.arange(c, dtype=np.float64)
    xi = np.broadcast_to((gamma[:, None] ** (idx + 1.0))[:, :, None], (RET_HEADS, c, RET_DK))
    ks = np.broadcast_to((RET_DK ** -0.5 * gamma[:, None] ** (-1.0 - idx))[:, :, None],
                         (RET_HEADS, c, RET_DK))
    chunk_decay = tuple(float(x) for x in gamma ** c)
    row = lambda wd: pl.BlockSpec((tm, wd), lambda i: (i, 0))
    tab0 = pl.BlockSpec((None, 1, LANES), lambda i: (i % nt, 0, 0))
    slabs = sw // SUBLANES
    rows = seq // LANES * TILE_PITCH
    out_shape = (
        jax.ShapeDtypeStruct((batch, slabs, rows, LANES), F32),
        jax.ShapeDtypeStruct((n, vw), BF16),
        jax.ShapeDtypeStruct((n, d), BF16), jax.ShapeDtypeStruct((n, d), BF16),
    )
    return pl.pallas_call(
        functools.partial(_in_proj_body, chunk_decay=chunk_decay, tiles_per_seq=nt),
        grid=(n // tm,),
        in_specs=[row(d), _layer(nw, layer), _layer(wuT, layer), _layer(w, layer),
                  _whole(cos.shape), _whole(sin.shape), tab0, tab0,
                  _whole(xi.shape), _whole(ks.shape)],
        out_specs=(pl.BlockSpec((None, slabs, tm // LANES * TILE_PITCH, LANES),
                                lambda i: (i // nt, 0, i % nt, 0)),
                   row(vw), row(d), row(d)),
        out_shape=out_shape,
        scratch_shapes=[pltpu.VMEM((RET_HEADS, RET_DK, RET_DV), F32)],
        compiler_params=_params(1),
        name="in_proj",
    )(h, nw, wuT, w, cos, sin, cos0, sin0, jnp.asarray(xi, F32), jnp.asarray(ks, F32))


def _s5_body(u_ref, prow_ref, bx_ref, cx_ref, d_ref, y_ref, taps_ref, m_ref, *, scan_steps):
    slabs_per_group = SSM_GROUP // SUBLANES
    for gg in range(prow_ref.shape[0]):
        _s5_group(u_ref, prow_ref.at[gg], bx_ref.at[gg], cx_ref.at[gg], d_ref.at[gg], y_ref,
                  taps_ref.at[gg], m_ref.at[gg], slab0=gg * slabs_per_group, scan_steps=scan_steps)


def _s5_group(u_ref, prow_ref, bx_ref, cx_ref, d_ref, y_ref, taps_ref, m_ref, *, slab0, scan_steps):
    batch = u_ref.shape[0]
    n = u_ref.shape[2] // TILE_PITCH
    t = S5_CHUNK
    half = LANES // 2
    lane = lax.broadcasted_iota(jnp.int32, (1, LANES), 1)
    lo = lane < half
    sgn = jnp.where(lo, -1.0, 1.0)

    def c_form(x):
        return jnp.where(lo, x, pltpu.roll(x, half, 1))

    def s_form(x):
        return jnp.where(lo, -pltpu.roll(x, half, 1), x)

    lr, li = prow_ref[0:1, :], prow_ref[1:2, :]
    step = jnp.exp(prow_ref[2:3, :])
    mag = jnp.exp(lr * step)
    ar, ai = mag * jnp.cos(li * step), mag * jnp.sin(li * step)
    den = lr * lr + li * li
    nr = ar - 1.0
    kr = (nr * lr + ai * li) / den
    ki = (ai * lr - nr * li) / den
    squares = [(ar, ai * sgn)]
    for _ in range(S5_CHUNK_BITS + scan_steps):
        c, s = squares[-1]
        squares.append((c * c - s * s, 2.0 * c * s))

    def powers(expo):
        x = jnp.broadcast_to(jnp.where(lo, 1.0, 0.0), (t, LANES))
        for j in range(S5_CHUNK_BITS):
            x = jnp.where(((expo >> j) & 1) == 1, _rot_half(x, *squares[j]), x)
        return x

    row_id = lax.broadcasted_iota(jnp.int32, (t, LANES), 0)
    a_tau = powers(row_id)
    a_rev = powers(t - 1 - row_id)
    a_next = _rot_half(a_tau, *squares[0])

    bbar = _rot_half(bx_ref[...], kr, ki * sgn)
    bb_c, bb_s = c_form(bbar), s_form(bbar)
    cmat = cx_ref[...]
    cc_c, cc_s = c_form(cmat), s_form(cmat)

    g1 = jnp.concatenate(
        [_rot_half(cmat, bb_c[ci:ci + 1], bb_s[ci:ci + 1]) * (-sgn) for ci in range(SSM_GROUP)], axis=0)
    taps_ref[...] = _dot_nt(g1, a_tau, precision=lax.Precision.HIGHEST)
    bs = jnp.concatenate(
        [_rot_half(a_rev, bb_c[ci:ci + 1], bb_s[ci:ci + 1]) for ci in range(SSM_GROUP)],
        axis=0).astype(BF16)
    cxt = jnp.concatenate(
        [_rot_half(a_next, cc_c[co:co + 1], cc_s[co:co + 1]) * (-sgn) for co in range(SSM_GROUP)],
        axis=0).astype(BF16)

    def channel(ref, b, c):
        return ref.at[b, slab0 + c // SUBLANES, pl.ds(c % SUBLANES, n, stride=TILE_PITCH), :]

    u = jnp.concatenate(
        [jnp.concatenate([channel(u_ref, b, c)[...].astype(BF16) for c in range(SSM_GROUP)], axis=1)
         for b in range(batch)], axis=0)
    x = _dot(u, bs)
    rows = lax.broadcasted_iota(jnp.int32, x.shape, 0) % n
    for k in range(scan_steps):
        sh = 1 << k
        prev = jnp.where(rows >= sh, pltpu.roll(x, sh, 0), 0.0)
        x = x + _rot_half(prev, *squares[S5_CHUNK_BITS + k])
    h_prev = jnp.where(rows >= 1, pltpu.roll(x, 1, 0), 0.0).astype(BF16)

    causal = (lax.broadcasted_iota(jnp.int32, (t, t), 1)
              >= lax.broadcasted_iota(jnp.int32, (t, t), 0))
    pair = 2 * t
    for b in range(batch):
        for slab in range(slab0, slab0 + SSM_GROUP // SUBLANES):
            for pad in range(SUBLANES, TILE_PITCH):
                y_ref[b, slab, pl.ds(pad, n, stride=TILE_PITCH), :] = jnp.zeros((n, LANES), F32)
    k_split = 4
    per_slice = SSM_GROUP // k_split
    for jp in range(SSM_GROUP // 2):
        y2 = _dot_nt(h_prev, cxt[jp * pair:(jp + 1) * pair, :])
        for ks in range(k_split):
            for ci in range(ks * per_slice, (ks + 1) * per_slice):
                for co in range(2 * jp, 2 * jp + 2):
                    tap_row = taps_ref[ci * SSM_GROUP + co:ci * SSM_GROUP + co + 1, :]
                    blk = pltpu.roll(jnp.broadcast_to(tap_row, (t, t)), 0, 1, stride=1, stride_axis=0)
                    m_ref[ci * t:(ci + 1) * t, co * t:(co + 1) * t] = (
                        jnp.where(causal, blk, 0.0).astype(BF16))
            rows_k = slice(ks * per_slice * t, (ks + 1) * per_slice * t)
            y2 = y2 + _dot(u[:, rows_k], m_ref[rows_k, jp * pair:(jp + 1) * pair])
        for co in range(2 * jp, 2 * jp + 2):
            for b in range(batch):
                skip = d_ref[co:co + 1, :] * channel(u_ref, b, co)[...]
                y = y2[b * n:(b + 1) * n, (co - 2 * jp) * t:(co - 2 * jp + 1) * t] + skip
                channel(y_ref, b, co)[...] = jax.nn.gelu(y)


def _s5(u4, prow, bx, cx, dB):
    batch, slabs, rows, _ = u4.shape
    groups = prow.shape[0]
    n_chunks = rows // TILE_PITCH
    scan_steps = max(n_chunks - 1, 0).bit_length()
    width = SSM_GROUP * S5_CHUNK
    gps = S5_GROUPS_PER_STEP
    per = slabs // groups * gps
    grp = lambda *shape: pl.BlockSpec((gps,) + shape, lambda g: (g,) + (0,) * len(shape))
    io = pl.BlockSpec((batch, per, rows, LANES), lambda g: (0, g, 0, 0))
    return pl.pallas_call(
        functools.partial(_s5_body, scan_steps=scan_steps),
        grid=(groups // gps,),
        in_specs=[io, grp(3, LANES), grp(SSM_GROUP, LANES), grp(SSM_GROUP, LANES),
                  grp(SSM_GROUP, LANES)],
        out_specs=io,
        out_shape=jax.ShapeDtypeStruct(u4.shape, F32),
        scratch_shapes=[pltpu.VMEM((gps, SSM_GROUP * SSM_GROUP, S5_CHUNK), F32),
                        pltpu.VMEM((gps, width, width), BF16)],
        compiler_params=_params(1),
        name="s5",
    )(u4, prow, bx, cx, dB)


def _mix_body(h_ref, y_ref, yb_ref, ga_ref, gb_ref, wgT_ref, bg_ref, wpa_ref, wpb_ref, wo_ref, o_ref):
    slabs = y_ref.shape[0]
    chunks = y_ref.shape[1] // TILE_PITCH
    yT = jnp.concatenate(
        [jnp.concatenate([y_ref[r, j * TILE_PITCH:j * TILE_PITCH + SUBLANES, :] for j in range(chunks)],
                         axis=1)
         for r in range(slabs)], axis=0)
    gate = jax.nn.sigmoid(_dot(wgT_ref[...], yT.astype(BF16)) + bg_ref[...])
    yaT = (yT * gate).astype(BF16)
    merged = (jax.nn.sigmoid(ga_ref[...].astype(F32)) * _dot_tn(yaT, wpa_ref[...])
              + jax.nn.sigmoid(gb_ref[...].astype(F32)) * _dot(yb_ref[...], wpb_ref[...]))
    o_ref[...] = h_ref[...] + _dot(merged.astype(BF16), wo_ref[...])


def _mix(h, y4, yb, ga, gb, wgT, bg, wpa, wpb, wo, layer, batch, seq):
    n, d = h.shape
    tm = min(MIX_TILE, seq)
    nt = seq // tm
    slabs = y4.shape[1]
    row = lambda w: pl.BlockSpec((tm, w), lambda i: (i, 0))
    stacks = (wgT, bg, wpa, wpb, wo)
    return pl.pallas_call(
        _mix_body,
        grid=(n // tm,),
        in_specs=[row(d),
                  pl.BlockSpec((None, slabs, tm // LANES * TILE_PITCH, LANES),
                               lambda i: (i // nt, 0, i % nt, 0)),
                  row(yb.shape[1]), row(d), row(d)]
        + [_layer(w, layer) for w in stacks],
        out_specs=row(d),
        out_shape=jax.ShapeDtypeStruct((n, d), F32),
        compiler_params=_params(1),
        name="mix",
    )(h, y4, yb, ga, gb, *stacks)


def _ffn_body(h_ref, nm_ref, wup_ref, wdn_ref, nf_ref, o_ref, *, final):
    h = h_ref[...]
    z = _rmsnorm(h, nm_ref[...]).astype(BF16)
    ff = wup_ref.shape[1]
    step = 1024
    for j in range(0, ff, step):
        a = jnp.square(jnp.maximum(_dot(z, wup_ref[:, j:j + step]), 0.0)).astype(BF16)
        h = h + _dot(a, wdn_ref[j:j + step, :])
    if final:
        h = _rmsnorm(h, nf_ref[...])
    o_ref[...] = h


def _ffn(h, nm, wup, wdn, nf, layer, seq, final):
    n, d = h.shape
    tm = min(FFN_TILE, seq)
    row = pl.BlockSpec((tm, d), lambda i: (i, 0))
    return pl.pallas_call(
        functools.partial(_ffn_body, final=final),
        grid=(n // tm,),
        in_specs=[row, _layer(nm, layer), _layer(wup, layer), _layer(wdn, layer), _whole(nf.shape)],
        out_specs=row,
        out_shape=jax.ShapeDtypeStruct((n, d), F32),
        compiler_params=_params(1),
        name="ffn",
    )(h, nm, wup, wdn, nf)


def _rotary_tables(pos):
    inv_freq = ROPE_BASE ** (-jnp.arange(0, RET_DK, 2, dtype=F32) / RET_DK)
    ang = pos[:, None] * inv_freq[None, :]
    cos, sin = jnp.cos(ang), jnp.sin(ang)
    return jnp.concatenate([cos, cos], axis=1), jnp.concatenate([-sin, sin], axis=1)


def _dup(x):
    return jnp.concatenate([x, x], axis=-1)


def kernel(x, w_in, lam_re, lam_im, b_re, b_im, c_re, c_im, d_skip, log_step, w_glu, b_glu,
           w_proj_ssm, w_proj_ret, w_out, norm_mix, norm_mlp, w_up, w_down, norm_final):
    batch, seq, d = x.shape
    depth, groups, p = lam_re.shape
    sw = w_glu.shape[1]
    assert p == SSM_STATE and sw == groups * SSM_GROUP and 2 * p == LANES
    assert seq % S5_CHUNK == 0
    assert seq % min(IN_PROJ_TILE, seq) == 0 and seq % min(MIX_TILE, seq) == 0 and seq % min(FFN_TILE, seq) == 0
    assert min(IN_PROJ_TILE, seq) % min(RET_CHUNK, seq) == 0
    h = x.reshape(batch * seq, d).astype(F32)
    w_all = w_in.astype(BF16)
    wuT_all = jnp.swapaxes(w_all[:, :, :sw], 1, 2)
    wgT_all = jnp.swapaxes(w_glu, 1, 2).astype(BF16)
    wpa_all, wpb_all, wo_all = (w.astype(BF16) for w in (w_proj_ssm, w_proj_ret, w_out))
    wup_all, wdn_all = w_up.astype(BF16), w_down.astype(BF16)
    nmix_all = norm_mix.reshape(depth, 1, d).astype(F32)
    nmlp_all = norm_mlp.reshape(depth, 1, d).astype(F32)
    bg_all = b_glu.reshape(depth, sw, 1).astype(F32)
    for l in range(depth):
        u4, yb, ga, gb = _in_proj(h, nmix_all, wuT_all, w_all, l, batch, seq)
        prow = jnp.stack([_dup(lam_re[l]), _dup(lam_im[l]),
                          jnp.broadcast_to(log_step[l][:, None], (groups, LANES))], axis=1).astype(F32)
        bx = jnp.concatenate([b_re[l], b_im[l]], axis=1).transpose(0, 2, 1).astype(F32)
        cx = jnp.concatenate([c_re[l], c_im[l]], axis=2).astype(F32)
        dB = jnp.broadcast_to(d_skip[l].reshape(groups, SSM_GROUP, 1), (groups, SSM_GROUP, LANES))
        y4 = _s5(u4, prow, bx, cx, dB.astype(F32))
        h = _mix(h, y4, yb, ga, gb, wgT_all, bg_all, wpa_all, wpb_all, wo_all, l, batch, seq)
        h = _ffn(h, nmlp_all, wup_all, wdn_all, norm_final.reshape(1, d).astype(F32), l, seq,
                 final=(l == depth - 1))
    return h.reshape(batch, seq, d).astype(x.dtype)
```

```python
import functools

import numpy as np
import jax
import jax.numpy as jnp
from jax import lax
from jax.experimental import pallas as pl
from jax.experimental.pallas import tpu as pltpu

F32 = jnp.float32
BF16 = jnp.bfloat16

EPS = 1e-6
SSM_GROUP = 16
SSM_STATE = 64
RET_HEADS = 4
RET_DK = 128
RET_DV = 256
ROPE_BASE = 10000.0

LANES = 128
SUBLANES = 8
TILE_PITCH = 10
S5_CHUNK = LANES
S5_CHUNK_BITS = 7
S5_GROUPS_PER_STEP = 2
RET_CHUNK = 256
IN_PROJ_TILE = 1024
MIX_TILE = 1024
FFN_TILE = 1024
VMEM_LIMIT = 56 * 1024 * 1024


def _params(n_axes, vmem=VMEM_LIMIT):
    return pltpu.CompilerParams(
        dimension_semantics=("arbitrary",) * n_axes, vmem_limit_bytes=vmem)


def _whole(shape):
    zeros = (0,) * len(shape)
    return pl.BlockSpec(shape, lambda *_: zeros, pipeline_mode=pl.Buffered(1))


def _layer(stack, l):
    zeros = (0,) * (stack.ndim - 1)
    return pl.BlockSpec((None,) + stack.shape[1:], lambda *_: (l,) + zeros,
                        pipeline_mode=pl.Buffered(1))


def _dot(a, b):
    return jnp.dot(a, b, preferred_element_type=F32)


def _dot_nt(a, b, precision=None):
    return lax.dot_general(a, b, (((1,), (1,)), ((), ())), precision=precision,
                           preferred_element_type=F32)


def _dot_tn(a, b):
    return lax.dot_general(a, b, (((0,), (0,)), ((), ())), preferred_element_type=F32)


def _rmsnorm(x, w):
    return x * lax.rsqrt(jnp.mean(x * x, axis=-1, keepdims=True) + EPS) * w


def _rot_half(x, c, s):
    return x * c + pltpu.roll(x, LANES // 2, 1) * s


def _in_proj_body(h_ref, nw_ref, wuT_ref, w_ref, cos_ref, sin_ref, cos0_ref, sin0_ref, xi_ref, ks_ref,
                  u_ref, o_ref, ga_ref, gb_ref, state_ref, *, chunk_decay, tiles_per_seq):
    @pl.when(pl.program_id(0) % tiles_per_seq == 0)
    def _():
        state_ref[...] = jnp.zeros_like(state_ref)

    z = _rmsnorm(h_ref[...], nw_ref[...]).astype(BF16)
    uT = _dot_nt(wuT_ref[...], z)
    chunks = uT.shape[1] // LANES
    for r in range(uT.shape[0] // SUBLANES):
        for pad in range(SUBLANES, TILE_PITCH):
            u_ref[r, pl.ds(pad, chunks, stride=TILE_PITCH), :] = jnp.zeros((chunks, LANES), F32)
        for j in range(chunks):
            u_ref[r, j * TILE_PITCH:j * TILE_PITCH + SUBLANES, :] = (
                uT[r * SUBLANES:(r + 1) * SUBLANES, j * LANES:(j + 1) * LANES])
    cos = cos_ref[...] * cos0_ref[...] - sin_ref[...] * sin0_ref[...]
    sin = sin_ref[...] * cos0_ref[...] + cos_ref[...] * sin0_ref[...]
    base = uT.shape[0]
    qk = RET_HEADS * RET_DK
    vw = RET_HEADS * RET_DV
    d = ga_ref.shape[1]
    q_all = _dot(z, w_ref[:, base:base + qk])
    k_all = _dot(z, w_ref[:, base + qk:base + 2 * qk])
    base += 2 * qk
    v_all = _dot(z, w_ref[:, base:base + vw]).astype(BF16)
    g_all = _dot(z, w_ref[:, base + vw:base + 2 * vw])
    base += 2 * vw
    gate_dots = [(ga_ref, base), (gb_ref, base + d)]

    c = xi_ref.shape[1]
    tm = z.shape[0]
    causal = (lax.broadcasted_iota(jnp.int32, (c, c), 0)
              >= lax.broadcasted_iota(jnp.int32, (c, c), 1))
    for sub in range(tm // c):
        rows = slice(sub * c, (sub + 1) * c)
        for hh in range(RET_HEADS):
            qk_cols = slice(hh * RET_DK, (hh + 1) * RET_DK)
            v_cols = slice(hh * RET_DV, (hh + 1) * RET_DV)
            q = (_rot_half(q_all[rows, qk_cols], cos[rows], sin[rows]) * xi_ref[hh]).astype(BF16)
            k = (_rot_half(k_all[rows, qk_cols], cos[rows], sin[rows]) * ks_ref[hh]).astype(BF16)
            v = v_all[rows, v_cols]
            state = state_ref[hh]
            scores = jnp.where(causal, _dot_nt(q, k), 0.0).astype(BF16)
            o = _dot(scores, v) + _dot(q, state.astype(BF16))
            state_ref[hh] = chunk_decay[hh] * (state + _dot_tn(k, v))
            o = o * lax.rsqrt(jnp.mean(o * o, axis=-1, keepdims=True) + EPS)
            o_ref[rows, v_cols] = (o * jax.nn.silu(g_all[rows, v_cols])).astype(BF16)
        if gate_dots:
            gate_ref, col = gate_dots.pop(0)
            gate_ref[...] = _dot(z, w_ref[:, col:col + d]).astype(BF16)
    for gate_ref, col in gate_dots:
        gate_ref[...] = _dot(z, w_ref[:, col:col + d]).astype(BF16)


def _in_proj(h, nw, wuT, w, layer, batch, seq):
    n, d = h.shape
    tm = min(IN_PROJ_TILE, seq)
    nt = seq // tm
    cos, sin = _rotary_tables(jnp.arange(tm, dtype=F32))
    cos0, sin0 = _rotary_tables(jnp.arange(nt, dtype=F32) * tm)
    cos0, sin0 = cos0.reshape(nt, 1, LANES), sin0.reshape(nt, 1, LANES)
    sw = wuT.shape[1]
    vw = RET_HEADS * RET_DV
    c = min(RET_CHUNK, tm)
    gamma = 1.0 - 2.0 ** (-5.0 - np.arange(RET_HEADS, dtype=np.float64))
    idx = np.arange(c, dtype=np.float64)
    xi = np.broadcast_to((gamma[:, None] ** (idx + 1.0))[:, :, None], (RET_HEADS, c, RET_DK))
    ks = np.broadcast_to((RET_DK ** -0.5 * gamma[:, None] ** (-1.0 - idx))[:, :, None],
                         (RET_HEADS, c, RET_DK))
    chunk_decay = tuple(float(x) for x in gamma ** c)
    row = lambda wd: pl.BlockSpec((tm, wd), lambda i: (i, 0))
    tab0 = pl.BlockSpec((None, 1, LANES), lambda i: (i % nt, 0, 0))
    slabs = sw // SUBLANES
    rows = seq // LANES * TILE_PITCH
    out_shape = (
        jax.ShapeDtypeStruct((batch, slabs, rows, LANES), F32),
        jax.ShapeDtypeStruct((n, vw), BF16),
        jax.ShapeDtypeStruct((n, d), BF16), jax.ShapeDtypeStruct((n, d), BF16),
    )
    return pl.pallas_call(
        functools.partial(_in_proj_body, chunk_decay=chunk_decay, tiles_per_seq=nt),
        grid=(n // tm,),
        in_specs=[row(d), _layer(nw, layer), _layer(wuT, layer), _layer(w, layer),
                  _whole(cos.shape), _whole(sin.shape), tab0, tab0,
                  _whole(xi.shape), _whole(ks.shape)],
        out_specs=(pl.BlockSpec((None, slabs, tm // LANES * TILE_PITCH, LANES),
                                lambda i: (i // nt, 0, i % nt, 0)),
                   row(vw), row(d), row(d)),
        out_shape=out_shape,
        scratch_shapes=[pltpu.VMEM((RET_HEADS, RET_DK, RET_DV), F32)],
        compiler_params=_params(1),
        name="in_proj",
    )(h, nw, wuT, w, cos, sin, cos0, sin0, jnp.asarray(xi, F32), jnp.asarray(ks, F32))


def _s5_body(u_ref, prow_ref, bx_ref, cx_ref, d_ref, y_ref, taps_ref, m_ref, *, scan_steps):
    slabs_per_group = SSM_GROUP // SUBLANES
    for gg in range(prow_ref.shape[0]):
        _s5_group(u_ref, prow_ref.at[gg], bx_ref.at[gg], cx_ref.at[gg], d_ref.at[gg], y_ref,
                  taps_ref.at[gg], m_ref.at[gg], slab0=gg * slabs_per_group, scan_steps=scan_steps)


def _s5_group(u_ref, prow_ref, bx_ref, cx_ref, d_ref, y_ref, taps_ref, m_ref, *, slab0, scan_steps):
    batch = u_ref.shape[0]
    n = u_ref.shape[2] // TILE_PITCH
    t = S5_CHUNK
    half = LANES // 2
    lane = lax.broadcasted_iota(jnp.int32, (1, LANES), 1)
    lo = lane < half
    sgn = jnp.where(lo, -1.0, 1.0)

    def c_form(x):
        return jnp.where(lo, x, pltpu.roll(x, half, 1))

    def s_form(x):
        return jnp.where(lo, -pltpu.roll(x, half, 1), x)

    lr, li = prow_ref[0:1, :], prow_ref[1:2, :]
    step = jnp.exp(prow_ref[2:3, :])
    mag = jnp.exp(lr * step)
    ar, ai = mag * jnp.cos(li * step), mag * jnp.sin(li * step)
    den = lr * lr + li * li
    nr = ar - 1.0
    kr = (nr * lr + ai * li) / den
    ki = (ai * lr - nr * li) / den
    squares = [(ar, ai * sgn)]
    for _ in range(S5_CHUNK_BITS + scan_steps):
        c, s = squares[-1]
        squares.append((c * c - s * s, 2.0 * c * s))

    def powers(expo):
        x = jnp.broadcast_to(jnp.where(lo, 1.0, 0.0), (t, LANES))
        for j in range(S5_CHUNK_BITS):
            x = jnp.where(((expo >> j) & 1) == 1, _rot_half(x, *squares[j]), x)
        return x

    row_id = lax.broadcasted_iota(jnp.int32, (t, LANES), 0)
    a_tau = powers(row_id)
    a_rev = powers(t - 1 - row_id)
    a_next = _rot_half(a_tau, *squares[0])

    bbar = _rot_half(bx_ref[...], kr, ki * sgn)
    bb_c, bb_s = c_form(bbar), s_form(bbar)
    cmat = cx_ref[...]
    cc_c, cc_s = c_form(cmat), s_form(cmat)

    g1 = jnp.concatenate(
        [_rot_half(cmat, bb_c[ci:ci + 1], bb_s[ci:ci + 1]) * (-sgn) for ci in range(SSM_GROUP)], axis=0)
    taps_ref[...] = _dot_nt(g1, a_tau, precision=lax.Precision.HIGHEST)
    bs = jnp.concatenate(
        [_rot_half(a_rev, bb_c[ci:ci + 1], bb_s[ci:ci + 1]) for ci in range(SSM_GROUP)],
        axis=0).astype(BF16)
    cxt = jnp.concatenate(
        [_rot_half(a_next, cc_c[co:co + 1], cc_s[co:co + 1]) * (-sgn) for co in range(SSM_GROUP)],
        axis=0).astype(BF16)

    def channel(ref, b, c):
        return ref.at[b, slab0 + c // SUBLANES, pl.ds(c % SUBLANES, n, stride=TILE_PITCH), :]

    u = jnp.concatenate(
        [jnp.concatenate([channel(u_ref, b, c)[...].astype(BF16) for c in range(SSM_GROUP)], axis=1)
         for b in range(batch)], axis=0)
    x = _dot(u, bs)
    rows = lax.broadcasted_iota(jnp.int32, x.shape, 0) % n
    for k in range(scan_steps):
        sh = 1 << k
        prev = jnp.where(rows >= sh, pltpu.roll(x, sh, 0), 0.0)
        x = x + _rot_half(prev, *squares[S5_CHUNK_BITS + k])
    h_prev = jnp.where(rows >= 1, pltpu.roll(x, 1, 0), 0.0).astype(BF16)

    causal = (lax.broadcasted_iota(jnp.int32, (t, t), 1)
              >= lax.broadcasted_iota(jnp.int32, (t, t), 0))
    pair = 2 * t
    for b in range(batch):
        for slab in range(slab0, slab0 + SSM_GROUP // SUBLANES):
            for pad in range(SUBLANES, TILE_PITCH):
                y_ref[b, slab, pl.ds(pad, n, stride=TILE_PITCH), :] = jnp.zeros((n, LANES), F32)
    for jp in range(SSM_GROUP // 2):
        for ci in range(SSM_GROUP):
            for co in range(2 * jp, 2 * jp + 2):
                tap_row = taps_ref[ci * SSM_GROUP + co:ci * SSM_GROUP + co + 1, :]
                blk = pltpu.roll(jnp.broadcast_to(tap_row, (t, t)), 0, 1, stride=1, stride_axis=0)
                m_ref[ci * t:(ci + 1) * t, co * t:(co + 1) * t] = (
                    jnp.where(causal, blk, 0.0).astype(BF16))
        y2 = (_dot(u, m_ref[:, jp * pair:(jp + 1) * pair])
              + _dot_nt(h_prev, cxt[jp * pair:(jp + 1) * pair, :]))
        for co in range(2 * jp, 2 * jp + 2):
            for b in range(batch):
                skip = d_ref[co:co + 1, :] * channel(u_ref, b, co)[...]
                y = y2[b * n:(b + 1) * n, (co - 2 * jp) * t:(co - 2 * jp + 1) * t] + skip
                channel(y_ref, b, co)[...] = jax.nn.gelu(y)


def _s5(u4, prow, bx, cx, dB):
    batch, slabs, rows, _ = u4.shape
    groups = prow.shape[0]
    n_chunks = rows // TILE_PITCH
    scan_steps = max(n_chunks - 1, 0).bit_length()
    width = SSM_GROUP * S5_CHUNK
    gps = S5_GROUPS_PER_STEP
    per = slabs // groups * gps
    grp = lambda *shape: pl.BlockSpec((gps,) + shape, lambda g: (g,) + (0,) * len(shape))
    io = pl.BlockSpec((batch, per, rows, LANES), lambda g: (0, g, 0, 0))
    return pl.pallas_call(
        functools.partial(_s5_body, scan_steps=scan_steps),
        grid=(groups // gps,),
        in_specs=[io, grp(3, LANES), grp(SSM_GROUP, LANES), grp(SSM_GROUP, LANES),
                  grp(SSM_GROUP, LANES)],
        out_specs=io,
        out_shape=jax.ShapeDtypeStruct(u4.shape, F32),
        scratch_shapes=[pltpu.VMEM((gps, SSM_GROUP * SSM_GROUP, S5_CHUNK), F32),
                        pltpu.VMEM((gps, width, width), BF16)],
        compiler_params=_params(1),
        name="s5",
    )(u4, prow, bx, cx, dB)


def _mix_body(h_ref, y_ref, yb_ref, ga_ref, gb_ref, wgT_ref, bg_ref, wpa_ref, wpb_ref, wo_ref, o_ref):
    slabs = y_ref.shape[0]
    chunks = y_ref.shape[1] // TILE_PITCH
    yT = jnp.concatenate(
        [jnp.concatenate([y_ref[r, j * TILE_PITCH:j * TILE_PITCH + SUBLANES, :] for j in range(chunks)],
                         axis=1)
         for r in range(slabs)], axis=0)
    gate = jax.nn.sigmoid(_dot(wgT_ref[...], yT.astype(BF16)) + bg_ref[...])
    yaT = (yT * gate).astype(BF16)
    merged = (jax.nn.sigmoid(ga_ref[...].astype(F32)) * _dot_tn(yaT, wpa_ref[...])
              + jax.nn.sigmoid(gb_ref[...].astype(F32)) * _dot(yb_ref[...], wpb_ref[...]))
    o_ref[...] = h_ref[...] + _dot(merged.astype(BF16), wo_ref[...])


def _mix(h, y4, yb, ga, gb, wgT, bg, wpa, wpb, wo, layer, batch, seq):
    n, d = h.shape
    tm = min(MIX_TILE, seq)
    nt = seq // tm
    slabs = y4.shape[1]
    row = lambda w: pl.BlockSpec((tm, w), lambda i: (i, 0))
    stacks = (wgT, bg, wpa, wpb, wo)
    return pl.pallas_call(
        _mix_body,
        grid=(n // tm,),
        in_specs=[row(d),
                  pl.BlockSpec((None, slabs, tm // LANES * TILE_PITCH, LANES),
                               lambda i: (i // nt, 0, i % nt, 0)),
                  row(yb.shape[1]), row(d), row(d)]
        + [_layer(w, layer) for w in stacks],
        out_specs=row(d),
        out_shape=jax.ShapeDtypeStruct((n, d), F32),
        compiler_params=_params(1),
        name="mix",
    )(h, y4, yb, ga, gb, *stacks)


def _ffn_body(h_ref, nm_ref, wup_ref, wdn_ref, nf_ref, o_ref, *, final):
    h = h_ref[...]
    z = _rmsnorm(h, nm_ref[...]).astype(BF16)
    ff = wup_ref.shape[1]
    step = 1024
    for j in range(0, ff, step):
        a = jnp.square(jnp.maximum(_dot(z, wup_ref[:, j:j + step]), 0.0)).astype(BF16)
        h = h + _dot(a, wdn_ref[j:j + step, :])
    if final:
        h = _rmsnorm(h, nf_ref[...])
    o_ref[...] = h


def _ffn(h, nm, wup, wdn, nf, layer, seq, final):
    n, d = h.shape
    tm = min(FFN_TILE, seq)
    row = pl.BlockSpec((tm, d), lambda i: (i, 0))
    return pl.pallas_call(
        functools.partial(_ffn_body, final=final),
        grid=(n // tm,),
        in_specs=[row, _layer(nm, layer), _layer(wup, layer), _layer(wdn, layer), _whole(nf.shape)],
        out_specs=row,
        out_shape=jax.ShapeDtypeStruct((n, d), F32),
        compiler_params=_params(1),
        name="ffn",
    )(h, nm, wup, wdn, nf)


def _rotary_tables(pos):
    inv_freq = ROPE_BASE ** (-jnp.arange(0, RET_DK, 2, dtype=F32) / RET_DK)
    ang = pos[:, None] * inv_freq[None, :]
    cos, sin = jnp.cos(ang), jnp.sin(ang)
    return jnp.concatenate([cos, cos], axis=1), jnp.concatenate([-sin, sin], axis=1)


def _dup(x):
    return jnp.concatenate([x, x], axis=-1)


def kernel(x, w_in, lam_re, lam_im, b_re, b_im, c_re, c_im, d_skip, log_step, w_glu, b_glu,
           w_proj_ssm, w_proj_ret, w_out, norm_mix, norm_mlp, w_up, w_down, norm_final):
    batch, seq, d = x.shape
    depth, groups, p = lam_re.shape
    sw = w_glu.shape[1]
    assert p == SSM_STATE and sw == groups * SSM_GROUP and 2 * p == LANES
    assert seq % S5_CHUNK == 0
    assert seq % min(IN_PROJ_TILE, seq) == 0 and seq % min(MIX_TILE, seq) == 0 and seq % min(FFN_TILE, seq) == 0
    assert min(IN_PROJ_TILE, seq) % min(RET_CHUNK, seq) == 0
    h = x.reshape(batch * seq, d).astype(F32)
    w_all = w_in.astype(BF16)
    wuT_all = jnp.swapaxes(w_all[:, :, :sw], 1, 2)
    wgT_all = jnp.swapaxes(w_glu, 1, 2).astype(BF16)
    wpa_all, wpb_all, wo_all = (w.astype(BF16) for w in (w_proj_ssm, w_proj_ret, w_out))
    wup_all, wdn_all = w_up.astype(BF16), w_down.astype(BF16)
    nmix_all = norm_mix.reshape(depth, 1, d).astype(F32)
    nmlp_all = norm_mlp.reshape(depth, 1, d).astype(F32)
    bg_all = b_glu.reshape(depth, sw, 1).astype(F32)
    for l in range(depth):
        u4, yb, ga, gb = _in_proj(h, nmix_all, wuT_all, w_all, l, batch, seq)
        prow = jnp.stack([_dup(lam_re[l]), _dup(lam_im[l]),
                          jnp.broadcast_to(log_step[l][:, None], (groups, LANES))], axis=1).astype(F32)
        bx = jnp.concatenate([b_re[l], b_im[l]], axis=1).transpose(0, 2, 1).astype(F32)
        cx = jnp.concatenate([c_re[l], c_im[l]], axis=2).astype(F32)
        dB = jnp.broadcast_to(d_skip[l].reshape(groups, SSM_GROUP, 1), (groups, SSM_GROUP, LANES))
        y4 = _s5(u4, prow, bx, cx, dB.astype(F32))
        h = _mix(h, y4, yb, ga, gb, wgT_all, bg_all, wpa_all, wpb_all, wo_all, l, batch, seq)
        h = _ffn(h, nmlp_all, wup_all, wdn_all, norm_final.reshape(1, d).astype(F32), l, seq,
                 final=(l == depth - 1))
    return h.reshape(batch, seq, d).astype(x.dtype)
```
